```python
import math
import jax, jax.numpy as jnp
from jax import lax
import numpy as np

D_MODEL = 2048
BATCH = 8
SEQ = 2048
DEPTH = 1
DEC_BATCH = 32
DEC_SEQ = 4
PAST_LEN = 8192
PAGE_SIZE = 128

SSM_EXPAND = 2
D_INNER = SSM_EXPAND * D_MODEL
SSM_HEAD_DIM = 64
SSM_HEADS = D_INNER // SSM_HEAD_DIM
SSM_GROUPS = 8
SSM_HPG = SSM_HEADS // SSM_GROUPS
D_STATE = 128
CONV_W = 4
CONV_DIM = D_INNER + 2 * SSM_GROUPS * D_STATE
SSD_CHUNK = 64
DT_MIN = 1e-3
DT_MAX = 1e-1
ATT_HEAD_DIM = 128
DIL_GROUPS = ((128, 1), (512, 4), (2048, 16))
N_DIL = len(DIL_GROUPS)
ATT_HPG = 4
ATT_HEADS = N_DIL * ATT_HPG
ATT_QKV_WIDTH = ATT_HEADS * ATT_HEAD_DIM
ATT_OUT_WIDTH = ATT_HPG * ATT_HEAD_DIM
ATT_BLOCK = 128
ALIBI_MAX_BIAS = 8.0
D_FF = ((8 * D_MODEL // 3 + 255) // 256) * 256
N_SUB = 3
IN_SIZES = (D_INNER, CONV_DIM, SSM_HEADS, ATT_QKV_WIDTH, ATT_QKV_WIDTH, ATT_QKV_WIDTH, 2 * D_MODEL)
N_IN = sum(IN_SIZES)
EPS = 1e-6
NEG_INF = -1e30

kernel_name = 'hybrid_ssd_dilated_swa_macaron_step'


def rmsnorm(x, g):
    xf = x.astype(jnp.float32)
    y = xf * lax.rsqrt(jnp.mean(xf * xf, axis=-1, keepdims=True) + EPS)
    return (y * g.astype(jnp.float32)).astype(x.dtype)


def split_at(x, sizes):
    return jnp.split(x, np.cumsum(sizes)[:-1].tolist(), axis=-1)


def swiglu(h, w_gu, w_down):
    g, u = jnp.split(h @ w_gu, 2, axis=-1)
    return (jax.nn.silu(g) * u) @ w_down


def alibi_slopes():
    i = np.arange(1, ATT_HEADS + 1, dtype=np.float32)
    s = np.exp2(-ALIBI_MAX_BIAS * i / ATT_HEADS).astype(np.float32)
    return jnp.asarray(s).reshape(N_DIL, ATT_HPG)


def causal_dwconv(xbc, prev, w, b):
    xp = jnp.concatenate([prev.astype(xbc.dtype), xbc], axis=1)
    y = lax.conv_general_dilated(xp, w.astype(xp.dtype)[:, None, :], (1,), 'VALID',
                                 dimension_numbers=('NWC', 'WIO', 'NWC'),
                                 feature_group_count=CONV_DIM)
    return y + b.astype(y.dtype), xp[:, -(CONV_W - 1):]


def ssd(x, dt, a, bm, cm, h0):
    f32 = jnp.float32
    Bsz, T = x.shape[:2]
    q = min(SSD_CHUNK, T)
    nc = -(-T // q)
    pad = nc * q - T

    def to_chunks(t):
        t = jnp.pad(t.astype(f32), [(0, 0), (0, pad)] + [(0, 0)] * (t.ndim - 2))
        return t.reshape((Bsz, nc, q) + t.shape[2:])

    xc = to_chunks(x).reshape(Bsz, nc, q, SSM_GROUPS, SSM_HPG, SSM_HEAD_DIM)
    dtc = to_chunks(dt).reshape(Bsz, nc, q, SSM_GROUPS, SSM_HPG)
    bc = to_chunks(bm)
    cc = to_chunks(cm)
    la = jnp.moveaxis(dtc * a.reshape(SSM_GROUPS, SSM_HPG), 2, -1)
    cum = jnp.cumsum(la, axis=-1)
    seg = cum[..., :, None] - cum[..., None, :]
    causal = jnp.tril(jnp.ones((q, q), dtype=bool))
    Lmat = jnp.exp(jnp.where(causal, seg, -jnp.inf))
    xdt = xc * dtc[..., None]
    cb = jnp.einsum('bcign,bcjgn->bcgij', cc, bc)
    y_diag = jnp.einsum('bcgij,bcgrij,bcjgrp->bcigrp', cb, Lmat, xdt)
    decay_end = jnp.exp(cum[..., -1:] - cum)
    states = jnp.einsum('bcjgn,bcgrj,bcjgrp->bcgrpn', bc, decay_end, xdt)
    chunk_decay = jnp.exp(cum[..., -1])

    def step(h, inp):
        s, dec = inp
        return h * dec[..., None, None] + s, h

    h_init = h0.astype(f32).reshape(Bsz, SSM_GROUPS, SSM_HPG, SSM_HEAD_DIM, D_STATE)
    h_last, h_prev = lax.scan(step, h_init, (jnp.moveaxis(states, 1, 0), jnp.moveaxis(chunk_decay, 1, 0)))
    h_prev = jnp.moveaxis(h_prev, 0, 1)
    y_off = jnp.einsum('bcign,bcgri,bcgrpn->bcigrp', cc, jnp.exp(cum), h_prev)
    y = (y_diag + y_off).reshape(Bsz, nc * q, SSM_HEADS, SSM_HEAD_DIM)[:, :T]
    return y, h_last.reshape(Bsz, SSM_HEADS, SSM_HEAD_DIM, D_STATE)


def dilated_attn_prompt(q, k, v, window, dil, slopes):
    f32 = jnp.float32
    Bsz, S, H, Dh = q.shape
    M = S // dil
    nb = -(-M // ATT_BLOCK)
    Mp = nb * ATT_BLOCK
    kmax = window // dil

    def residues(t):
        t = t.reshape(Bsz, M, dil, H, Dh).transpose(0, 2, 1, 3, 4)
        return jnp.pad(t, ((0, 0), (0, 0), (0, Mp - M), (0, 0), (0, 0)))

    def band(t):
        t = jnp.pad(residues(t), ((0, 0), (0, 0), (ATT_BLOCK, 0), (0, 0), (0, 0)))
        t = t.reshape(Bsz, dil, nb + 1, ATT_BLOCK, H, Dh)
        return jnp.concatenate([t[:, :, :-1], t[:, :, 1:]], axis=3)

    qb = residues(q).reshape(Bsz, dil, nb, ATT_BLOCK, H, Dh)
    kb, vb = band(k), band(v)
    s = jnp.einsum('brnqhd,brnkhd->brnhqk', qb, kb, preferred_element_type=f32) * (Dh ** -0.5)
    qi = jnp.arange(ATT_BLOCK)[:, None]
    ki = jnp.arange(2 * ATT_BLOCK)[None, :]
    dist = qi + ATT_BLOCK - ki
    key_idx = (jnp.arange(nb)[:, None, None] - 1) * ATT_BLOCK + ki[None]
    valid = (dist >= 0) & (dist <= kmax) & (key_idx >= 0)
    s = s - slopes[:, None, None] * (dil * dist).astype(f32)
    s = jnp.where(valid[:, None], s, NEG_INF)
    m = jnp.max(s, axis=-1, keepdims=True)
    pr = jnp.exp(s - m)
    l = jnp.sum(pr, axis=-1, keepdims=True)
    o = jnp.einsum('brnhqk,brnkhd->brnqhd', pr, vb.astype(f32)) / jnp.moveaxis(l, 3, 4)
    lse = (m + jnp.log(l))[..., 0]
    o = o.reshape(Bsz, dil, Mp, H, Dh)[:, :, :M].transpose(0, 2, 1, 3, 4).reshape(Bsz, S, H, Dh)
    lse = jnp.swapaxes(lse, 3, 4).reshape(Bsz, dil, Mp, H)[:, :, :M].transpose(0, 2, 1, 3).reshape(Bsz, S, H)
    return o, lse


def dilated_attn_sample(q, k, v, kv_buf, window, dil, slopes):
    f32 = jnp.float32
    T, Dh = q.shape[1], q.shape[3]
    lb = kv_buf.shape[1]
    kmax = window // dil
    kc = jnp.concatenate([kv_buf[:, :, 0].astype(k.dtype), k], axis=1)
    vc = jnp.concatenate([kv_buf[:, :, 1].astype(v.dtype), v], axis=1)
    steps = jnp.arange(kmax + 1)
    idx = lb + jnp.arange(T)[:, None] - dil * steps[None, :]
    valid = idx >= 0
    idx = jnp.maximum(idx, 0)
    kg = kc[:, idx].astype(f32)
    vg = vc[:, idx].astype(f32)
    s = jnp.einsum('bthd,btkhd->bthk', q.astype(f32), kg) * (Dh ** -0.5)
    s = s - slopes[:, None] * (dil * steps).astype(f32)[None, :]
    s = jnp.where(valid[:, None, :], s, NEG_INF)
    m = jnp.max(s, axis=-1, keepdims=True)
    pr = jnp.exp(s - m)
    l = jnp.sum(pr, axis=-1, keepdims=True)
    o = jnp.einsum('bthk,btkhd->bthd', pr, vg) / l
    lse = (m + jnp.log(l))[..., 0]
    new_buf = jnp.stack([kc, vc], axis=2)[:, -lb:]
    return o, lse, new_buf


def mixer(h, conv_prev, ssm_prev, kv_bufs, p):
    f32 = jnp.float32
    Bsz, T, _ = h.shape
    prompt = kv_bufs is None
    if prompt:
        conv_prev = jnp.zeros((Bsz, CONV_W - 1, CONV_DIM), h.dtype)
        ssm_prev = jnp.zeros((Bsz, SSM_HEADS, SSM_HEAD_DIM, D_STATE), f32)
    z, xbc, dt_raw, q, k, v, gates = split_at(h @ p['w_in'], IN_SIZES)

    xbc, conv_new = causal_dwconv(xbc, conv_prev, p['conv_w'], p['conv_b'])
    xs, bm, cm = split_at(jax.nn.silu(xbc), (D_INNER, SSM_GROUPS * D_STATE, SSM_GROUPS * D_STATE))
    xs = xs.reshape(Bsz, T, SSM_HEADS, SSM_HEAD_DIM)
    bm = bm.reshape(Bsz, T, SSM_GROUPS, D_STATE)
    cm = cm.reshape(Bsz, T, SSM_GROUPS, D_STATE)
    dt = jax.nn.softplus(dt_raw.astype(f32) + p['dt_bias'].astype(f32))
    a = -jnp.exp(p['a_log'].astype(f32))
    y, ssm_new = ssd(xs, dt, a, bm, cm, ssm_prev)
    y = y + xs.astype(f32) * p['d_skip'].astype(f32)[:, None]
    y = y.reshape(Bsz, T, D_INNER).astype(h.dtype)
    y = rmsnorm(y * jax.nn.silu(z), p['g_ssm_norm'])
    branch_ssm = y @ p['w_ssm_proj']

    shp = (Bsz, T, N_DIL, ATT_HPG, ATT_HEAD_DIM)
    q, k, v = q.reshape(shp), k.reshape(shp), v.reshape(shp)
    slopes = alibi_slopes()
    outs, lses, kv_new = [], [], []
    for gi, (window, dil) in enumerate(DIL_GROUPS):
        qg, kg, vg = q[:, :, gi], k[:, :, gi], v[:, :, gi]
        if prompt:
            o, lse = dilated_attn_prompt(qg, kg, vg, window, dil, slopes[gi])
            buf = jnp.stack([kg, vg], axis=2)[:, -min(window, T):]
        else:
            o, lse, buf = dilated_attn_sample(qg, kg, vg, kv_bufs[gi], window, dil, slopes[gi])
        outs.append(o)
        lses.append(lse)
        kv_new.append(buf)
    wts = jax.nn.softmax(jnp.stack(lses, axis=0), axis=0)
    att = jnp.sum(wts[..., None] * jnp.stack(outs, axis=0), axis=0)
    att = att.reshape(Bsz, T, ATT_OUT_WIDTH).astype(h.dtype)
    branch_att = att @ p['w_att_proj']

    g_ssm, g_att = jnp.split(jax.nn.sigmoid(gates), 2, axis=-1)
    out = (g_ssm * branch_ssm + g_att * branch_att) @ p['w_out']
    return out, conv_new, ssm_new.astype(h.dtype), tuple(kv_new)


def layer(x, c, conv_prev, ssm_prev, kv_bufs, p):
    Bsz = x.shape[0]
    mod = (jax.nn.silu(c) @ p['w_ada'] + p['b_ada']).reshape(Bsz, N_SUB, 3, 1, D_MODEL)

    def pre(x, i, g):
        return rmsnorm(x, g) * (1 + mod[:, i, 1]) + mod[:, i, 0]

    def post(x, i, out, g, w):
        return x + w * mod[:, i, 2] * rmsnorm(out, g)

    x = post(x, 0, swiglu(pre(x, 0, p['g_pre_ffn1']), p['w_gu_ffn1'], p['w_down_ffn1']), p['g_post_ffn1'], 0.5)
    mix, conv_new, ssm_new, kv_new = mixer(pre(x, 1, p['g_pre_mix']), conv_prev, ssm_prev, kv_bufs, p)
    x = post(x, 1, mix, p['g_post_mix'], 1.0)
    x = post(x, 2, swiglu(pre(x, 2, p['g_pre_ffn2']), p['w_gu_ffn2'], p['w_down_ffn2']), p['g_post_ffn2'], 0.5)
    return x, conv_new, ssm_new, kv_new


def setup_inputs(seed: int = 0) -> dict:
    key = jax.random.key(seed)
    ks = iter(jax.random.split(key, 40))
    f32 = jnp.float32
    L = DEPTH

    def nrm(shape, scale):
        return jax.random.normal(next(ks), shape, f32) * scale

    def gain(shape):
        return 1.0 + nrm(shape, 0.1)

    dt0 = jnp.exp(jax.random.uniform(next(ks), (L, SSM_HEADS), f32, math.log(DT_MIN), math.log(DT_MAX)))
    dt_bias = dt0 + jnp.log(-jnp.expm1(-dt0))
    a_log = jnp.log(jax.random.uniform(next(ks), (L, SSM_HEADS), f32, 1.0, 16.0))
    kvshape = lambda w: (L, DEC_BATCH, min(w, PAST_LEN), 2, ATT_HPG, ATT_HEAD_DIM)
    return {
        'x_prompt': nrm((BATCH, SEQ, D_MODEL), 1.0),
        'x_sample': nrm((DEC_BATCH, DEC_SEQ, D_MODEL), 1.0),
        'c_prompt': nrm((BATCH, D_MODEL), 1.0),
        'c_sample': nrm((DEC_BATCH, D_MODEL), 1.0),
        'state_ssm': nrm((L, DEC_BATCH, SSM_HEADS, SSM_HEAD_DIM, D_STATE), 0.1),
        'state_conv': nrm((L, DEC_BATCH, CONV_W - 1, CONV_DIM), 1.0),
        'cache_kv_w128': nrm(kvshape(128), 1.0),
        'cache_kv_w512': nrm(kvshape(512), 1.0),
        'cache_kv_w2048': nrm(kvshape(2048), 1.0),
        'w_ada': nrm((L, D_MODEL, N_SUB * 3 * D_MODEL), D_MODEL ** -0.5),
        'b_ada': nrm((L, N_SUB * 3 * D_MODEL), 0.02),
        'g_pre_ffn1': gain((L, D_MODEL)),
        'g_post_ffn1': gain((L, D_MODEL)),
        'w_gu_ffn1': nrm((L, D_MODEL, 2 * D_FF), D_MODEL ** -0.5),
        'w_down_ffn1': nrm((L, D_FF, D_MODEL), D_FF ** -0.5),
        'g_pre_mix': gain((L, D_MODEL)),
        'g_post_mix': gain((L, D_MODEL)),
        'w_in': nrm((L, D_MODEL, N_IN), D_MODEL ** -0.5),
        'conv_w': nrm((L, CONV_W, CONV_DIM), CONV_W ** -0.5),
        'conv_b': nrm((L, CONV_DIM), 0.02),
        'dt_bias': dt_bias,
        'a_log': a_log,
        'd_skip': gain((L, SSM_HEADS)),
        'g_ssm_norm': gain((L, D_INNER)),
        'w_ssm_proj': nrm((L, D_INNER, D_MODEL), D_INNER ** -0.5),
        'w_att_proj': nrm((L, ATT_OUT_WIDTH, D_MODEL), ATT_OUT_WIDTH ** -0.5),
        'w_out': nrm((L, D_MODEL, D_MODEL), D_MODEL ** -0.5),
        'g_pre_ffn2': gain((L, D_MODEL)),
        'g_post_ffn2': gain((L, D_MODEL)),
        'w_gu_ffn2': nrm((L, D_MODEL, 2 * D_FF), D_MODEL ** -0.5),
        'w_down_ffn2': nrm((L, D_FF, D_MODEL), D_FF ** -0.5),
    }


def reference(x_prompt, x_sample, c_prompt, c_sample, state_ssm, state_conv, cache_kv_w128, cache_kv_w512,
              cache_kv_w2048, w_ada, b_ada, g_pre_ffn1, g_post_ffn1, w_gu_ffn1, w_down_ffn1, g_pre_mix,
              g_post_mix, w_in, conv_w, conv_b, dt_bias, a_log, d_skip, g_ssm_norm, w_ssm_proj, w_att_proj,
              w_out, g_pre_ffn2, g_post_ffn2, w_gu_ffn2, w_down_ffn2):
    kv_in = (cache_kv_w128, cache_kv_w512, cache_kv_w2048)
    yp, ys = x_prompt, x_sample
    ssm_p, ssm_s, conv_p, conv_s = [], [], [], []
    kv_p = [[] for _ in DIL_GROUPS]
    kv_s = [[] for _ in DIL_GROUPS]
    for l in range(DEPTH):
        p = {'w_ada': w_ada[l], 'b_ada': b_ada[l],
             'g_pre_ffn1': g_pre_ffn1[l], 'g_post_ffn1': g_post_ffn1[l],
             'w_gu_ffn1': w_gu_ffn1[l], 'w_down_ffn1': w_down_ffn1[l],
             'g_pre_mix': g_pre_mix[l], 'g_post_mix': g_post_mix[l], 'w_in': w_in[l],
             'conv_w': conv_w[l], 'conv_b': conv_b[l], 'dt_bias': dt_bias[l], 'a_log': a_log[l],
             'd_skip': d_skip[l], 'g_ssm_norm': g_ssm_norm[l], 'w_ssm_proj': w_ssm_proj[l],
             'w_att_proj': w_att_proj[l], 'w_out': w_out[l],
             'g_pre_ffn2': g_pre_ffn2[l], 'g_post_ffn2': g_post_ffn2[l],
             'w_gu_ffn2': w_gu_ffn2[l], 'w_down_ffn2': w_down_ffn2[l]}
        yp, cp, sp, kp = layer(yp, c_prompt, None, None, None, p)
        ys, cs, ss, ksm = layer(ys, c_sample, state_conv[l], state_ssm[l], tuple(kv[l] for kv in kv_in), p)
        ssm_p.append(sp)
        ssm_s.append(ss)
        conv_p.append(cp)
        conv_s.append(cs)
        for gi in range(N_DIL):
            kv_p[gi].append(kp[gi])
            kv_s[gi].append(ksm[gi])
    return (yp, ys, jnp.stack(ssm_p), jnp.stack(ssm_s), jnp.stack(conv_p), jnp.stack(conv_s),
            jnp.stack(kv_p[0]), jnp.stack(kv_s[0]), jnp.stack(kv_p[1]), jnp.stack(kv_s[1]),
            jnp.stack(kv_p[2]), jnp.stack(kv_s[2]))
```

```python
import functools
import math

import numpy as np
import jax
import jax.numpy as jnp
from jax import lax
from jax.experimental import pallas as pl
from jax.experimental.pallas import tpu as pltpu

D_MODEL = 2048
SSM_HEAD_DIM = 64
D_INNER = 2 * D_MODEL
SSM_HEADS = D_INNER // SSM_HEAD_DIM
SSM_GROUPS = 8
SSM_HPG = SSM_HEADS // SSM_GROUPS
D_STATE = 128
CONV_W = 4
CONV_DIM = D_INNER + 2 * SSM_GROUPS * D_STATE
ATT_HEAD_DIM = 128
DIL_GROUPS = ((128, 1), (512, 4), (2048, 16))
N_DIL = len(DIL_GROUPS)
ATT_HPG = 4
ATT_HEADS = N_DIL * ATT_HPG
ATT_QKV_WIDTH = ATT_HEADS * ATT_HEAD_DIM
ATT_OUT_WIDTH = ATT_HPG * ATT_HEAD_DIM
ATT_BLOCK = 128
ALIBI_MAX_BIAS = 8.0
D_FF = ((8 * D_MODEL // 3 + 255) // 256) * 256
N_SUB = 3
EPS = 1e-6
NEG_INF = -1e30

LANES = 128
SUBLANES = 8
VMEM_LIMIT = 56 * 1024 * 1024
SSD_Q = 128

F32 = jnp.float32
BF16 = jnp.bfloat16


def _cparams(*sem):
    return pltpu.CompilerParams(dimension_semantics=sem, vmem_limit_bytes=VMEM_LIMIT)


def _alibi_slopes():
    i = np.arange(1, ATT_HEADS + 1, dtype=np.float32)
    return np.exp2(-ALIBI_MAX_BIAS * i / ATT_HEADS).astype(np.float32).reshape(N_DIL, ATT_HPG)


def _silu(x):
    return x * (1.0 / (1.0 + jnp.exp(-x)))


def _sigmoid(x):
    return 1.0 / (1.0 + jnp.exp(-x))


def _rms(x, g):
    return x * lax.rsqrt(jnp.mean(x * x, axis=-1, keepdims=True) + EPS) * g


def _split2(x):
    hi = x.astype(BF16)
    lo = (x - hi.astype(F32)).astype(BF16)
    return hi, lo


def _split3(x):
    hi = x.astype(BF16)
    r = x - hi.astype(F32)
    mid = r.astype(BF16)
    lo = (r - mid.astype(F32)).astype(BF16)
    return hi, mid, lo


def _dot(a, b):
    return jnp.dot(a, b, preferred_element_type=F32)


def _dot_nt(a, b):
    return lax.dot_general(a, b, (((1,), (1,)), ((), ())), preferred_element_type=F32)


def _dot_tn(a, b):
    return lax.dot_general(a, b, (((0,), (0,)), ((), ())), preferred_element_type=F32)


def _mod_spec(mod, bm, rows_per_batch):
    if mod.ndim == 3:
        bpb = rows_per_batch // bm
        return pl.BlockSpec((None, 1, D_MODEL), lambda i, *_: (i // bpb, 0, 0))
    return pl.BlockSpec((bm, D_MODEL), lambda i, *_: (i, 0))


def _vec_spec(n):
    return pl.BlockSpec((1, n), lambda *_: (0, 0))


def _ada_kernel(c_ref, w_ref, b_ref, o_ref):
    c = _silu(c_ref[...]).astype(BF16)
    o_ref[...] = _dot(c, w_ref[...].astype(BF16)) + b_ref[...]


def _ada(c, w, b):
    m, k = c.shape
    n = w.shape[1]
    bn = 1024
    return pl.pallas_call(
        _ada_kernel,
        grid=(n // bn,),
        in_specs=[pl.BlockSpec((m, k), lambda j: (0, 0)),
                  pl.BlockSpec((k, bn), lambda j: (0, j)),
                  pl.BlockSpec((1, bn), lambda j: (0, j))],
        out_specs=pl.BlockSpec((m, bn), lambda j: (0, j)),
        out_shape=jax.ShapeDtypeStruct((m, n), F32),
        compiler_params=_cparams("parallel"),
        name="ada",
    )(c, w, b.reshape(1, n))


def _ffn_kernel(x_ref, sh_ref, sc_ref, gt_ref, gpre_ref, gpost_ref, wg_ref, wu_ref, wd_ref, o_ref,
                h_scr, acc_scr):
    j = pl.program_id(1)

    @pl.when(j == 0)
    def _():
        x = x_ref[...]
        h = _rms(x, gpre_ref[...]) * (1.0 + sc_ref[...]) + sh_ref[...]
        h_scr[...] = h.astype(BF16)
        acc_scr[...] = jnp.zeros_like(acc_scr)

    h = h_scr[...]
    g = _dot(h, wg_ref[...])
    u = _dot(h, wu_ref[...])
    a = (_silu(g) * u).astype(BF16)
    acc_scr[...] += _dot(a, wd_ref[...])

    @pl.when(j == pl.num_programs(1) - 1)
    def _():
        o_ref[...] = x_ref[...] + 0.5 * gt_ref[...] * _rms(acc_scr[...], gpost_ref[...])


def _ffn(x, shift, scale, gate, g_pre, g_post, w_gu, w_down, bm, rows_per_batch):
    m, d = x.shape
    ff = w_down.shape[0]
    bf = 512
    nj = ff // bf
    ms = lambda a: _mod_spec(a, bm, rows_per_batch)
    return pl.pallas_call(
        _ffn_kernel,
        grid=(m // bm, nj),
        in_specs=[pl.BlockSpec((bm, d), lambda i, j: (i, 0)),
                  ms(shift), ms(scale), ms(gate), _vec_spec(d), _vec_spec(d),
                  pl.BlockSpec((d, bf), lambda i, j: (0, j)),
                  pl.BlockSpec((d, bf), lambda i, j: (0, j + nj)),
                  pl.BlockSpec((bf, d), lambda i, j: (j, 0))],
        out_specs=pl.BlockSpec((bm, d), lambda i, j: (i, 0)),
        out_shape=jax.ShapeDtypeStruct((m, d), F32),
        scratch_shapes=[pltpu.VMEM((bm, d), BF16), pltpu.VMEM((bm, d), F32)],
        compiler_params=_cparams("parallel", "arbitrary"),
        name="ffn",
    )(x, shift, scale, gate, g_pre.reshape(1, d), g_post.reshape(1, d), w_gu, w_gu, w_down)


def _prenorm_kernel(x_ref, sh_ref, sc_ref, g_ref, o_ref):
    o_ref[...] = (_rms(x_ref[...], g_ref[...]) * (1.0 + sc_ref[...]) + sh_ref[...]).astype(o_ref.dtype)


def _prenorm(x, shift, scale, g, bm, rows_per_batch):
    m, d = x.shape
    ms = lambda a: _mod_spec(a, bm, rows_per_batch)
    return pl.pallas_call(
        _prenorm_kernel,
        grid=(m // bm,),
        in_specs=[pl.BlockSpec((bm, d), lambda i: (i, 0)), ms(shift), ms(scale), _vec_spec(d)],
        out_specs=pl.BlockSpec((bm, d), lambda i: (i, 0)),
        out_shape=jax.ShapeDtypeStruct((m, d), BF16),
        compiler_params=_cparams("parallel"),
        name="prenorm",
    )(x, shift, scale, g.reshape(1, d))


def _mm_kernel(x_ref, w_ref, o_ref):
    o_ref[...] = _dot(x_ref[...], w_ref[...]).astype(o_ref.dtype)


def _mm(x, w, out_dtype, bm, bn, name):
    m, k = x.shape
    n = w.shape[1]
    return pl.pallas_call(
        _mm_kernel,
        grid=(m // bm, n // bn),
        in_specs=[pl.BlockSpec((bm, k), lambda i, j: (i, 0)),
                  pl.BlockSpec((k, bn), lambda i, j: (0, j))],
        out_specs=pl.BlockSpec((bm, bn), lambda i, j: (i, j)),
        out_shape=jax.ShapeDtypeStruct((m, n), out_dtype),
        compiler_params=_cparams("parallel", "arbitrary"),
        name=name,
    )(x, w)


def _ssd_kernel(xbc_ref, z_ref, dtr_ref, prev_ref, h0_ref, cw_ref, cb_ref, dtb_ref, alog_ref,
                dskip_ref, gn_ref, e_ref, et_ref, y_ref, hout_ref,
                xp_scr, act_scr, y_scr, h_scr, dt_scr, *, q, t_valid, has_init):
    c = pl.program_id(1)
    tail = SUBLANES

    @pl.when(c == 0)
    def _():
        xp_scr[0:tail, :] = jnp.zeros((tail, CONV_DIM), F32)
        xp_scr[tail - (CONV_W - 1):tail, :] = prev_ref[...]
        if has_init:
            h_scr[...] = h0_ref[...]
        else:
            h_scr[...] = jnp.zeros_like(h_scr)

    tp = xbc_ref.shape[0]
    if tp < q:
        xp_scr[tail:tail + q, :] = jnp.zeros((q, CONV_DIM), F32)
        xp_scr[tail:tail + tp, :] = xbc_ref[...].astype(F32)
        dt_scr[...] = jnp.zeros_like(dt_scr)
        dt_scr[0:tp, :] = dtr_ref[...]
    else:
        xp_scr[tail:tail + q, :] = xbc_ref[...].astype(F32)
        dt_scr[...] = dtr_ref[...]

    slab = 512
    for s in range(CONV_DIM // slab):
        cs = slice(s * slab, (s + 1) * slab)
        acc = cb_ref[:, cs] + cw_ref[0:1, cs] * xp_scr[tail - 3:tail - 3 + q, cs]
        acc = acc + cw_ref[1:2, cs] * xp_scr[tail - 2:tail - 2 + q, cs]
        acc = acc + cw_ref[2:3, cs] * xp_scr[tail - 1:tail - 1 + q, cs]
        acc = acc + cw_ref[3:4, cs] * xp_scr[tail:tail + q, cs]
        act_scr[:, cs] = _silu(acc)
    xp_scr[0:tail, :] = xp_scr[q:q + tail, :]

    x = dt_scr[:, 0:SSM_HEADS] + dtb_ref[...]
    dt = jnp.maximum(x, 0.0) + jnp.log1p(jnp.exp(-jnp.abs(x)))
    if t_valid < q:
        rows = lax.broadcasted_iota(jnp.int32, (q, SSM_HEADS), 0)
        dt = jnp.where(rows < t_valid, dt, 0.0)
    a = -jnp.exp(alog_ref[...])
    la = dt * a
    ri = lax.broadcasted_iota(jnp.int32, (q, q), 0)
    ci = lax.broadcasted_iota(jnp.int32, (q, q), 1)
    causal = ri >= ci
    ltri = jnp.where(causal, 1.0, 0.0).astype(BF16)
    utri = jnp.where(ri <= ci, 1.0, 0.0).astype(BF16)
    parts = _split3(la)
    cum = _dot(ltri, parts[0]) + _dot(ltri, parts[1]) + _dot(ltri, parts[2])
    cum_t = _dot_tn(parts[0], utri) + _dot_tn(parts[1], utri) + _dot_tn(parts[2], utri)
    cum_last = cum[q - 1:q, :]
    dend = jnp.exp(cum_last - cum)
    ecum = jnp.exp(cum)

    stack = jnp.concatenate([dt, dt * dend, ecum], axis=0)
    s_hi, s_lo = _split2(stack)
    e = e_ref[...]
    exp3 = _dot(s_hi, e) + _dot(s_lo, e)
    dt_e = exp3[0:q]
    dtd_e = exp3[q:2 * q]
    ecum_e = exp3[2 * q:3 * q]
    xs = act_scr[:, 0:D_INNER]
    xdt = (xs * dt_e).astype(BF16)
    xdd = (xs * dtd_e).astype(BF16)

    cdl = jnp.broadcast_to(jnp.exp(cum_t[:, q - 1:q]), (SSM_HEADS, LANES))
    c_hi, c_lo = _split2(cdl)
    et = et_ref[...]
    cdec = _dot(et, c_hi) + _dot(et, c_lo)

    lane = lax.broadcasted_iota(jnp.int32, (q, LANES), 1)
    lo_half = lane < SSM_HEAD_DIM
    gw = SSM_HPG * SSM_HEAD_DIM
    for g in range(SSM_GROUPS):
        bg = act_scr[:, D_INNER + g * D_STATE:D_INNER + (g + 1) * D_STATE].astype(BF16)
        cg = act_scr[:, D_INNER + SSM_GROUPS * D_STATE + g * D_STATE:
                     D_INNER + SSM_GROUPS * D_STATE + (g + 1) * D_STATE].astype(BF16)
        cbm = _dot_nt(cg, bg)
        hg = h_scr[g * gw:(g + 1) * gw, :]
        yoff = _dot_nt(cg, hg.astype(BF16))
        for pr in range(SSM_HPG // 2):
            h0 = g * SSM_HPG + 2 * pr
            a0 = cbm * jnp.exp(jnp.where(causal, cum[:, h0:h0 + 1] - cum_t[h0:h0 + 1, :], NEG_INF))
            a1 = cbm * jnp.exp(jnp.where(causal, cum[:, h0 + 1:h0 + 2] - cum_t[h0 + 1:h0 + 2, :], NEG_INF))
            lhs = jnp.concatenate([a0, a1], axis=1).astype(BF16)
            ls = slice(h0 * SSM_HEAD_DIM, h0 * SSM_HEAD_DIM + LANES)
            x2 = xdt[:, ls]
            zero = jnp.zeros_like(x2)
            rhs = jnp.concatenate([jnp.where(lo_half, x2, zero), jnp.where(lo_half, zero, x2)], axis=0)
            yd = _dot(lhs, rhs)
            y_scr[:, ls] = yd + yoff[:, pr * LANES:(pr + 1) * LANES] * ecum_e[:, ls]
        st = _dot_tn(xdd[:, g * gw:(g + 1) * gw], bg)
        h_scr[g * gw:(g + 1) * gw, :] = hg * cdec[g * gw:(g + 1) * gw, :] + st

    y = y_scr[0:tp, :] + act_scr[0:tp, 0:D_INNER] * dskip_ref[...]
    y = _rms(y * _silu(z_ref[...].astype(F32)), gn_ref[...])
    y_ref[...] = y.astype(y_ref.dtype)

    @pl.when(c == pl.num_programs(1) - 1)
    def _():
        hout_ref[...] = h_scr[...]


def _ssd(xbc, z, dtr, conv_prev, h0, conv_w, conv_b, dt_bias, a_log, d_skip, g_norm):
    b, t, _ = xbc.shape
    q = SSD_Q
    t_valid = min(t, q)
    if t >= q:
        assert t % q == 0
        nc, bt = t // q, q
    else:
        bt = -(-t // SUBLANES) * SUBLANES
        padt = lambda v: jnp.pad(v.astype(F32), ((0, 0), (0, bt - t), (0, 0)))
        xbc, z, dtr = padt(xbc), padt(z), padt(dtr)
        nc = 1
    has_init = h0 is not None
    if not has_init:
        h0 = jnp.zeros((b, SUBLANES, D_STATE), F32)
    hp = SSM_HEADS * SSM_HEAD_DIM
    e = jnp.asarray(np.kron(np.eye(SSM_HEADS, dtype=np.float32), np.ones((1, SSM_HEAD_DIM), np.float32)), BF16)
    et = jnp.asarray(np.kron(np.eye(SSM_HEADS, dtype=np.float32), np.ones((SSM_HEAD_DIM, 1), np.float32)), BF16)
    dskip_e = jnp.repeat(d_skip, SSM_HEAD_DIM).reshape(1, D_INNER)
    kern = functools.partial(_ssd_kernel, q=q, t_valid=t_valid, has_init=has_init)
    h0_block = (None, hp, D_STATE) if has_init else (None, SUBLANES, D_STATE)
    y, h = pl.pallas_call(
        kern,
        grid=(b, nc),
        in_specs=[pl.BlockSpec((None, bt, CONV_DIM), lambda i, c: (i, c, 0)),
                  pl.BlockSpec((None, bt, D_INNER), lambda i, c: (i, c, 0)),
                  pl.BlockSpec((None, bt, LANES), lambda i, c: (i, c, 0)),
                  pl.BlockSpec((None, CONV_W - 1, CONV_DIM), lambda i, c: (i, 0, 0)),
                  pl.BlockSpec(h0_block, lambda i, c: (i, 0, 0)),
                  pl.BlockSpec((CONV_W, CONV_DIM), lambda i, c: (0, 0)),
                  _vec_spec(CONV_DIM), _vec_spec(SSM_HEADS), _vec_spec(SSM_HEADS),
                  _vec_spec(D_INNER), _vec_spec(D_INNER),
                  pl.BlockSpec((SSM_HEADS, D_INNER), lambda i, c: (0, 0)),
                  pl.BlockSpec((D_INNER, SSM_HEADS), lambda i, c: (0, 0))],
        out_specs=[pl.BlockSpec((None, bt, D_INNER), lambda i, c: (i, c, 0)),
                   pl.BlockSpec((None, hp, D_STATE), lambda i, c: (i, 0, 0))],
        out_shape=[jax.ShapeDtypeStruct((b, bt * nc, D_INNER), BF16),
                   jax.ShapeDtypeStruct((b, hp, D_STATE), F32)],
        scratch_shapes=[pltpu.VMEM((SUBLANES + q, CONV_DIM), F32),
                        pltpu.VMEM((q, CONV_DIM), F32),
                        pltpu.VMEM((q, D_INNER), F32),
                        pltpu.VMEM((hp, D_STATE), F32),
                        pltpu.VMEM((q, LANES), F32)],
        compiler_params=_cparams("parallel", "arbitrary"),
        name="ssd",
    )(xbc, z, dtr, conv_prev, h0, conv_w, conv_b.reshape(1, CONV_DIM), dt_bias.reshape(1, SSM_HEADS),
      a_log.reshape(1, SSM_HEADS), dskip_e, g_norm.reshape(1, D_INNER), e, et)
    return y[:, :t], h


def _attn_prompt_kernel(q_ref, k_ref, v_ref, o_ref, l_ref, *, dil, kmax, slopes, nb):
    blk = ATT_BLOCK
    qi = lax.broadcasted_iota(jnp.int32, (blk, 2 * blk), 0)
    ki = lax.broadcasted_iota(jnp.int32, (blk, 2 * blk), 1)
    dist = qi + blk - ki
    valid2 = (dist >= 0) & (dist <= kmax)
    distf = (dil * dist).astype(F32)
    scale = ATT_HEAD_DIM ** -0.5
    for h in range(ATT_HPG):
        hs = slice(h * ATT_HEAD_DIM, (h + 1) * ATT_HEAD_DIM)
        bias2 = slopes[h] * distf
        for n in range(nb):
            qb = q_ref[n * blk:(n + 1) * blk, hs]
            if n == 0:
                kb = k_ref[0:blk, hs]
                vb = v_ref[0:blk, hs]
                s = _dot_nt(qb, kb) * scale - bias2[:, blk:]
                s = jnp.where(valid2[:, blk:], s, NEG_INF)
            else:
                kb = k_ref[(n - 1) * blk:(n + 1) * blk, hs]
                vb = v_ref[(n - 1) * blk:(n + 1) * blk, hs]
                s = _dot_nt(qb, kb) * scale - bias2
                s = jnp.where(valid2, s, NEG_INF)
            m = jnp.max(s, axis=-1, keepdims=True)
            p = jnp.exp(s - m)
            l = jnp.sum(p, axis=-1, keepdims=True)
            o = _dot(p.astype(BF16), vb) / l
            o_ref[n * blk:(n + 1) * blk, hs] = o
            l_ref[n * blk:(n + 1) * blk, hs] = jnp.broadcast_to(m + jnp.log(l), (blk, ATT_HEAD_DIM))


def _attn_prompt(qkv, gi):
    b, s, w = qkv.shape
    window, dil = DIL_GROUPS[gi]
    m = s // dil
    assert m % ATT_BLOCK == 0
    nb = m // ATT_BLOCK
    cw = ATT_OUT_WIDTH
    per_r = w // cw
    qkv_r = qkv.reshape(b, m, dil * w)
    kern = functools.partial(_attn_prompt_kernel, dil=dil, kmax=window // dil,
                             slopes=[float(v) for v in _alibi_slopes()[gi]], nb=nb)
    spec = lambda off: pl.BlockSpec((None, m, cw), lambda i, r: (i, 0, r * per_r + off))
    o, l = pl.pallas_call(
        kern,
        grid=(b, dil),
        in_specs=[spec(gi), spec(N_DIL + gi), spec(2 * N_DIL + gi)],
        out_specs=[pl.BlockSpec((None, m, cw), lambda i, r: (i, 0, r))] * 2,
        out_shape=[jax.ShapeDtypeStruct((b, m, dil * cw), F32)] * 2,
        compiler_params=_cparams("parallel", "arbitrary"),
        name=f"attn_prompt_d{dil}",
    )(qkv_r, qkv_r, qkv_r)
    return o.reshape(b, s, cw), l.reshape(b, s, cw)


def _attn_sample_kernel(qkv_ref, c0_ref, c1_ref, c2_ref, o_ref, *, t, slopes):
    tp = SUBLANES
    scale = ATT_HEAD_DIM ** -0.5
    qkv = qkv_ref[...].astype(BF16)
    caches = (c0_ref, c1_ref, c2_ref)
    row_c = lax.broadcasted_iota(jnp.int32, (tp, ATT_BLOCK), 0)
    col_c = lax.broadcasted_iota(jnp.int32, (tp, ATT_BLOCK), 1)
    row_n = lax.broadcasted_iota(jnp.int32, (tp, tp), 0)
    col_n = lax.broadcasted_iota(jnp.int32, (tp, tp), 1)
    row_o = lax.broadcasted_iota(jnp.int32, (tp, ATT_HEAD_DIM), 0)
    kvw = 2 * ATT_OUT_WIDTH
    for h in range(ATT_HPG):
        outs, lses = [], []
        for gi, (window, dil) in enumerate(DIL_GROUPS):
            slope = slopes[gi][h]
            kmax = window // dil
            col = gi * ATT_OUT_WIDTH + h * ATT_HEAD_DIM
            qh = qkv[:, col:col + ATT_HEAD_DIM]
            kn = qkv[:, ATT_QKV_WIDTH + col:ATT_QKV_WIDTH + col + ATT_HEAD_DIM]
            vn = qkv[:, 2 * ATT_QKV_WIDTH + col:2 * ATT_QKV_WIDTH + col + ATT_HEAD_DIM]
            cref = caches[gi]
            s_new = _dot_nt(qh, kn) * scale
            if dil == 1:
                kc = cref[:, h * ATT_HEAD_DIM:(h + 1) * ATT_HEAD_DIM].astype(BF16)
                s_c = _dot_nt(qh, kc) * scale
                dist_c = kmax + row_c - col_c
                s_c = jnp.where((dist_c <= kmax), s_c - slope * dist_c.astype(F32), NEG_INF)
                dist_n = row_n - col_n
                s_new = jnp.where((dist_n >= 0) & (col_n < t), s_new - slope * dist_n.astype(F32), NEG_INF)
            else:
                s_c = jnp.zeros((tp, ATT_BLOCK), F32)
                for r in range(t):
                    kc = cref[:, r * kvw + h * ATT_HEAD_DIM:r * kvw + (h + 1) * ATT_HEAD_DIM].astype(BF16)
                    s_c = jnp.where(row_c == r, _dot_nt(qh, kc) * scale, s_c)
                dist_c = kmax - col_c
                s_c = s_c - (slope * dil) * dist_c.astype(F32)
                s_new = jnp.where((row_n == col_n) & (col_n < t), s_new, NEG_INF)
            m = jnp.maximum(jnp.max(s_c, axis=-1, keepdims=True), jnp.max(s_new, axis=-1, keepdims=True))
            p_c = jnp.exp(s_c - m)
            p_n = jnp.exp(s_new - m)
            l = jnp.sum(p_c, axis=-1, keepdims=True) + jnp.sum(p_n, axis=-1, keepdims=True)
            o = _dot(p_n.astype(BF16), vn)
            if dil == 1:
                vc = cref[:, ATT_OUT_WIDTH + h * ATT_HEAD_DIM:ATT_OUT_WIDTH + (h + 1) * ATT_HEAD_DIM].astype(BF16)
                o = o + _dot(p_c.astype(BF16), vc)
            else:
                pcb = p_c.astype(BF16)
                for r in range(t):
                    c0 = r * kvw + ATT_OUT_WIDTH + h * ATT_HEAD_DIM
                    vc = cref[:, c0:c0 + ATT_HEAD_DIM].astype(BF16)
                    o = o + jnp.where(row_o == r, _dot(pcb, vc), 0.0)
            outs.append(o / l)
            lses.append(m + jnp.log(l))
        mm = jnp.maximum(jnp.maximum(lses[0], lses[1]), lses[2])
        ws = [jnp.exp(x - mm) for x in lses]
        att = (ws[0] * outs[0] + ws[1] * outs[1] + ws[2] * outs[2]) / (ws[0] + ws[1] + ws[2])
        o_ref[:, h * ATT_HEAD_DIM:(h + 1) * ATT_HEAD_DIM] = att


def _attn_sample(qkv, caches):
    b, t, w = qkv.shape
    kvw = 2 * ATT_OUT_WIDTH
    assert t <= SUBLANES
    qkv = jnp.pad(qkv.astype(F32), ((0, 0), (0, SUBLANES - t), (0, 0)))
    specs = [pl.BlockSpec((None, SUBLANES, w), lambda i: (i, 0, 0))]
    views = []
    for (window, dil), cache in zip(DIL_GROUPS, caches):
        lb = cache.shape[1]
        assert lb == window and lb // dil == ATT_BLOCK and (dil == 1 or dil >= t)
        views.append(cache.reshape(b, lb // dil, dil * kvw))
        width = kvw if dil == 1 else t * kvw
        specs.append(pl.BlockSpec((None, lb // dil, width), lambda i: (i, 0, 0)))
    kern = functools.partial(_attn_sample_kernel, t=t,
                             slopes=[[float(v) for v in row] for row in _alibi_slopes()])
    att = pl.pallas_call(
        kern,
        grid=(b,),
        in_specs=specs,
        out_specs=pl.BlockSpec((None, SUBLANES, ATT_OUT_WIDTH), lambda i: (i, 0, 0)),
        out_shape=jax.ShapeDtypeStruct((b, SUBLANES, ATT_OUT_WIDTH), F32),
        compiler_params=_cparams("parallel"),
        name="attn_sample",
    )(qkv, *views)
    return att[:, :t].astype(BF16)


def _merge_kernel(*refs, n_att):
    att_refs = refs[:n_att]
    y_ref, gs_ref, ga_ref, ws_ref, wa_ref, o_ref, att_scr = refs[n_att:]
    j = pl.program_id(1)

    @pl.when(j == 0)
    def _():
        if n_att == 1:
            att_scr[...] = att_refs[0][...]
        else:
            o0, o1, o2, l0, l1, l2 = [r[...] for r in att_refs]
            mm = jnp.maximum(jnp.maximum(l0, l1), l2)
            w0, w1, w2 = jnp.exp(l0 - mm), jnp.exp(l1 - mm), jnp.exp(l2 - mm)
            att_scr[...] = ((w0 * o0 + w1 * o1 + w2 * o2) / (w0 + w1 + w2)).astype(BF16)

    bs = _dot(y_ref[...], ws_ref[...])
    ba = _dot(att_scr[...], wa_ref[...])
    o_ref[...] = (_sigmoid(gs_ref[...].astype(F32)) * bs + _sigmoid(ga_ref[...].astype(F32)) * ba).astype(o_ref.dtype)


def _merge(att_inputs, y, gates, w_ssm, w_att, bm):
    m = y.shape[0]
    bn = 512
    nj = D_MODEL // bn
    att_specs = [pl.BlockSpec((bm, ATT_OUT_WIDTH), lambda i, j: (i, 0)) for _ in att_inputs]
    return pl.pallas_call(
        functools.partial(_merge_kernel, n_att=len(att_inputs)),
        grid=(m // bm, nj),
        in_specs=att_specs + [pl.BlockSpec((bm, D_INNER), lambda i, j: (i, 0)),
                              pl.BlockSpec((bm, bn), lambda i, j: (i, j)),
                              pl.BlockSpec((bm, bn), lambda i, j: (i, j + nj)),
                              pl.BlockSpec((D_INNER, bn), lambda i, j: (0, j)),
                              pl.BlockSpec((ATT_OUT_WIDTH, bn), lambda i, j: (0, j))],
        out_specs=pl.BlockSpec((bm, bn), lambda i, j: (i, j)),
        out_shape=jax.ShapeDtypeStruct((m, D_MODEL), BF16),
        scratch_shapes=[pltpu.VMEM((bm, ATT_OUT_WIDTH), BF16)],
        compiler_params=_cparams("parallel", "arbitrary"),
        name="merge",
    )(*att_inputs, y, gates, gates, w_ssm, w_att)


def _outproj_kernel(mg_ref, x_ref, gt_ref, gpost_ref, w_ref, o_ref):
    out = _dot(mg_ref[...], w_ref[...])
    o_ref[...] = x_ref[...] + gt_ref[...] * _rms(out, gpost_ref[...])


def _outproj(merged, x, gate, g_post, w_out, bm, rows_per_batch):
    m, d = x.shape
    return pl.pallas_call(
        _outproj_kernel,
        grid=(m // bm,),
        in_specs=[pl.BlockSpec((bm, d), lambda i: (i, 0)),
                  pl.BlockSpec((bm, d), lambda i: (i, 0)),
                  _mod_spec(gate, bm, rows_per_batch), _vec_spec(d),
                  pl.BlockSpec((d, d), lambda i: (0, 0))],
        out_specs=pl.BlockSpec((bm, d), lambda i: (i, 0)),
        out_shape=jax.ShapeDtypeStruct((m, d), F32),
        compiler_params=_cparams("parallel"),
        name="outproj",
    )(merged, x, gate, g_post.reshape(1, d), w_out)


def _layer(x3, mod, conv_prev, ssm_prev, kv_bufs, p, bm):
    b, t, d = x3.shape
    m = b * t
    prompt = kv_bufs is None
    x = x3.reshape(m, d)
    if prompt:
        mods = [mod[:, k].reshape(b, 1, d) for k in range(3 * N_SUB)]
    else:
        mods = [jnp.repeat(mod[:, k], t, axis=0) for k in range(3 * N_SUB)]

    x = _ffn(x, mods[0], mods[1], mods[2], p['g_pre_ffn1'], p['g_post_ffn1'], p['w_gu_ffn1'],
             p['w_down_ffn1'], bm, t)

    h = _prenorm(x, mods[3], mods[4], p['g_pre_mix'], bm, t)
    bn = 512
    z = _mm(h, p['w_z'], BF16, bm, bn, "inproj_z")
    xbc = _mm(h, p['w_xbc'], BF16, bm, bn, "inproj_xbc")
    dtr = _mm(h, p['w_dt'], F32, bm, LANES, "inproj_dt")
    qkv = _mm(h, p['w_qkv'], BF16, bm, bn, "inproj_qkv")
    gates = _mm(h, p['w_gates'], BF16, bm, bn, "inproj_gates")

    xbc3 = xbc.reshape(b, t, CONV_DIM)
    if prompt:
        conv_prev = jnp.zeros((b, CONV_W - 1, CONV_DIM), F32)
        conv_new = xbc3[:, t - (CONV_W - 1):].astype(F32)
    else:
        conv_new = jnp.concatenate([conv_prev, xbc3.astype(F32)], axis=1)[:, -(CONV_W - 1):]
    h0 = None if prompt else ssm_prev.reshape(b, SSM_HEADS * SSM_HEAD_DIM, D_STATE)
    y, ssm_new = _ssd(xbc3, z.reshape(b, t, D_INNER), dtr.reshape(b, t, LANES), conv_prev, h0,
                      p['conv_w'], p['conv_b'], p['dt_bias'], p['a_log'], p['d_skip'], p['g_ssm_norm'])
    ssm_new = ssm_new.reshape(b, SSM_HEADS, SSM_HEAD_DIM, D_STATE)

    qkv3 = qkv.reshape(b, t, 3 * ATT_QKV_WIDTH)
    kv_new = []
    for gi, (window, dil) in enumerate(DIL_GROUPS):
        c0 = ATT_QKV_WIDTH + gi * ATT_OUT_WIDTH
        kg = qkv3[:, :, c0:c0 + ATT_OUT_WIDTH].astype(F32).reshape(b, t, ATT_HPG, ATT_HEAD_DIM)
        vg = qkv3[:, :, ATT_QKV_WIDTH + c0:ATT_QKV_WIDTH + c0 + ATT_OUT_WIDTH].astype(F32).reshape(
            b, t, ATT_HPG, ATT_HEAD_DIM)
        new = jnp.stack([kg, vg], axis=2)
        if prompt:
            kv_new.append(new[:, -min(window, t):])
        else:
            lb = kv_bufs[gi].shape[1]
            kv_new.append(jnp.concatenate([kv_bufs[gi], new], axis=1)[:, -lb:])
    if prompt:
        os_, ls_ = [], []
        for gi in range(N_DIL):
            o, l = _attn_prompt(qkv3, gi)
            os_.append(o.reshape(m, ATT_OUT_WIDTH))
            ls_.append(l.reshape(m, ATT_OUT_WIDTH))
        att_inputs = os_ + ls_
    else:
        caches = [kv.reshape(b, kv.shape[1], 2 * ATT_OUT_WIDTH) for kv in kv_bufs]
        att_inputs = [_attn_sample(qkv3, caches).reshape(m, ATT_OUT_WIDTH)]

    merged = _merge(att_inputs, y.reshape(m, D_INNER), gates, p['w_ssm_proj'], p['w_att_proj'], bm)
    x = _outproj(merged, x, mods[5], p['g_post_mix'], p['w_out'], bm, t)

    x = _ffn(x, mods[6], mods[7], mods[8], p['g_pre_ffn2'], p['g_post_ffn2'], p['w_gu_ffn2'],
             p['w_down_ffn2'], bm, t)
    return x.reshape(b, t, d), conv_new, ssm_new, kv_new


def _prep_weights(w_gu_ffn1, w_down_ffn1, w_in, w_ssm_proj, w_att_proj, w_out, w_gu_ffn2, w_down_ffn2):
    o = 0
    segs = {}
    for name, size in (('z', D_INNER), ('xbc', CONV_DIM), ('dt', SSM_HEADS), ('qkv', 3 * ATT_QKV_WIDTH),
                       ('gates', 2 * D_MODEL)):
        segs[name] = w_in[:, o:o + size].astype(BF16)
        o += size
    segs['dt'] = jnp.pad(segs['dt'], ((0, 0), (0, LANES - SSM_HEADS)))
    return {
        'w_gu_ffn1': w_gu_ffn1.astype(BF16), 'w_down_ffn1': w_down_ffn1.astype(BF16),
        'w_gu_ffn2': w_gu_ffn2.astype(BF16), 'w_down_ffn2': w_down_ffn2.astype(BF16),
        'w_z': segs['z'], 'w_xbc': segs['xbc'], 'w_dt': segs['dt'], 'w_qkv': segs['qkv'],
        'w_gates': segs['gates'],
        'w_ssm_proj': w_ssm_proj.astype(BF16), 'w_att_proj': w_att_proj.astype(BF16),
        'w_out': w_out.astype(BF16),
    }


def kernel(x_prompt, x_sample, c_prompt, c_sample, state_ssm, state_conv, cache_kv_w128, cache_kv_w512, cache_kv_w2048, w_ada, b_ada, g_pre_ffn1, g_post_ffn1, w_gu_ffn1, w_down_ffn1, g_pre_mix, g_post_mix, w_in, conv_w, conv_b, dt_bias, a_log, d_skip, g_ssm_norm, w_ssm_proj, w_att_proj, w_out, g_pre_ffn2, g_post_ffn2, w_gu_ffn2, w_down_ffn2):
    depth = w_ada.shape[0]
    kv_in = (cache_kv_w128, cache_kv_w512, cache_kv_w2048)
    bp = x_prompt.shape[0]
    yp, ys = x_prompt, x_sample
    outs = [[] for _ in range(10)]
    for l in range(depth):
        p = _prep_weights(w_gu_ffn1[l], w_down_ffn1[l], w_in[l], w_ssm_proj[l], w_att_proj[l], w_out[l],
                          w_gu_ffn2[l], w_down_ffn2[l])
        p.update({'g_pre_ffn1': g_pre_ffn1[l], 'g_post_ffn1': g_post_ffn1[l], 'g_pre_mix': g_pre_mix[l],
                  'g_post_mix': g_post_mix[l], 'g_pre_ffn2': g_pre_ffn2[l], 'g_post_ffn2': g_post_ffn2[l],
                  'conv_w': conv_w[l], 'conv_b': conv_b[l], 'dt_bias': dt_bias[l], 'a_log': a_log[l],
                  'd_skip': d_skip[l], 'g_ssm_norm': g_ssm_norm[l]})
        c_all = jnp.concatenate([c_prompt, c_sample], axis=0)
        mod = _ada(c_all, w_ada[l], b_ada[l]).reshape(c_all.shape[0], 3 * N_SUB, D_MODEL)
        yp, cp, sp, kp = _layer(yp, mod[:bp], None, None, None, p, 512)
        ys, cs, ss, ksm = _layer(ys, mod[bp:], state_conv[l], state_ssm[l], tuple(kv[l] for kv in kv_in), p,
                                 ys.shape[0] * ys.shape[1])
        for lst, val in zip(outs, (sp, ss, cp, cs, kp[0], ksm[0], kp[1], ksm[1], kp[2], ksm[2])):
            lst.append(val)
    return (yp, ys) + tuple(jnp.stack(o) for o in outs)
```

```python
import functools
import math

import numpy as np
import jax
import jax.numpy as jnp
from jax import lax
from jax.experimental import pallas as pl
from jax.experimental.pallas import tpu as pltpu

D_MODEL = 2048
SSM_HEAD_DIM = 64
D_INNER = 2 * D_MODEL
SSM_HEADS = D_INNER // SSM_HEAD_DIM
SSM_GROUPS = 8
SSM_HPG = SSM_HEADS // SSM_GROUPS
D_STATE = 128
CONV_W = 4
CONV_DIM = D_INNER + 2 * SSM_GROUPS * D_STATE
ATT_HEAD_DIM = 128
DIL_GROUPS = ((128, 1), (512, 4), (2048, 16))
N_DIL = len(DIL_GROUPS)
ATT_HPG = 4
ATT_HEADS = N_DIL * ATT_HPG
ATT_QKV_WIDTH = ATT_HEADS * ATT_HEAD_DIM
ATT_OUT_WIDTH = ATT_HPG * ATT_HEAD_DIM
ATT_BLOCK = 128
ALIBI_MAX_BIAS = 8.0
D_FF = ((8 * D_MODEL // 3 + 255) // 256) * 256
N_SUB = 3
EPS = 1e-6
NEG_INF = -1e30

LANES = 128
SUBLANES = 8
VMEM_LIMIT = 56 * 1024 * 1024
SSD_Q = 128

F32 = jnp.float32
BF16 = jnp.bfloat16


def _cparams(*sem):
    return pltpu.CompilerParams(dimension_semantics=sem, vmem_limit_bytes=VMEM_LIMIT)


def _alibi_slopes():
    i = np.arange(1, ATT_HEADS + 1, dtype=np.float32)
    return np.exp2(-ALIBI_MAX_BIAS * i / ATT_HEADS).astype(np.float32).reshape(N_DIL, ATT_HPG)


def _silu(x):
    return x * (1.0 / (1.0 + jnp.exp(-x)))


def _sigmoid(x):
    return 1.0 / (1.0 + jnp.exp(-x))


def _rms(x, g):
    return x * lax.rsqrt(jnp.mean(x * x, axis=-1, keepdims=True) + EPS) * g


def _split2(x):
    hi = x.astype(BF16)
    lo = (x - hi.astype(F32)).astype(BF16)
    return hi, lo


def _split3(x):
    hi = x.astype(BF16)
    r = x - hi.astype(F32)
    mid = r.astype(BF16)
    lo = (r - mid.astype(F32)).astype(BF16)
    return hi, mid, lo


def _dot(a, b):
    return jnp.dot(a, b, preferred_element_type=F32)


def _dot_nt(a, b):
    return lax.dot_general(a, b, (((1,), (1,)), ((), ())), preferred_element_type=F32)


def _dot_tn(a, b):
    return lax.dot_general(a, b, (((0,), (0,)), ((), ())), preferred_element_type=F32)


def _mod_spec(mod, bm, rows_per_batch):
    if mod.ndim == 3:
        bpb = rows_per_batch // bm
        return pl.BlockSpec((None, 1, D_MODEL), lambda i, *_: (i // bpb, 0, 0))
    return pl.BlockSpec((bm, D_MODEL), lambda i, *_: (i, 0))


def _vec_spec(n):
    return pl.BlockSpec((1, n), lambda *_: (0, 0))


def _ada_kernel(c_ref, w_ref, b_ref, o_ref):
    c = _silu(c_ref[...]).astype(BF16)
    o_ref[...] = _dot(c, w_ref[...].astype(BF16)) + b_ref[...]


def _ada(c, w, b):
    m, k = c.shape
    n = w.shape[1]
    bn = 1024
    return pl.pallas_call(
        _ada_kernel,
        grid=(n // bn,),
        in_specs=[pl.BlockSpec((m, k), lambda j: (0, 0)),
                  pl.BlockSpec((k, bn), lambda j: (0, j)),
                  pl.BlockSpec((1, bn), lambda j: (0, j))],
        out_specs=pl.BlockSpec((m, bn), lambda j: (0, j)),
        out_shape=jax.ShapeDtypeStruct((m, n), F32),
        compiler_params=_cparams("parallel"),
        name="ada",
    )(c, w, b.reshape(1, n))


def _ffn_kernel(x_ref, sh_ref, sc_ref, gt_ref, gpre_ref, gpost_ref, wg_ref, wu_ref, wd_ref, o_ref,
                h_scr, acc_scr):
    j = pl.program_id(1)

    @pl.when(j == 0)
    def _():
        x = x_ref[...]
        h = _rms(x, gpre_ref[...]) * (1.0 + sc_ref[...]) + sh_ref[...]
        h_scr[...] = h.astype(BF16)
        acc_scr[...] = jnp.zeros_like(acc_scr)

    h = h_scr[...]
    g = _dot(h, wg_ref[...])
    u = _dot(h, wu_ref[...])
    a = (_silu(g) * u).astype(BF16)
    acc_scr[...] += _dot(a, wd_ref[...])

    @pl.when(j == pl.num_programs(1) - 1)
    def _():
        o_ref[...] = x_ref[...] + 0.5 * gt_ref[...] * _rms(acc_scr[...], gpost_ref[...])


def _ffn(x, shift, scale, gate, g_pre, g_post, w_gu, w_down, bm, rows_per_batch):
    m, d = x.shape
    ff = w_down.shape[0]
    bf = 512
    nj = ff // bf
    ms = lambda a: _mod_spec(a, bm, rows_per_batch)
    return pl.pallas_call(
        _ffn_kernel,
        grid=(m // bm, nj),
        in_specs=[pl.BlockSpec((bm, d), lambda i, j: (i, 0)),
                  ms(shift), ms(scale), ms(gate), _vec_spec(d), _vec_spec(d),
                  pl.BlockSpec((d, bf), lambda i, j: (0, j)),
                  pl.BlockSpec((d, bf), lambda i, j: (0, j + nj)),
                  pl.BlockSpec((bf, d), lambda i, j: (j, 0))],
        out_specs=pl.BlockSpec((bm, d), lambda i, j: (i, 0)),
        out_shape=jax.ShapeDtypeStruct((m, d), F32),
        scratch_shapes=[pltpu.VMEM((bm, d), BF16), pltpu.VMEM((bm, d), F32)],
        compiler_params=_cparams("parallel", "arbitrary"),
        name="ffn",
    )(x, shift, scale, gate, g_pre.reshape(1, d), g_post.reshape(1, d), w_gu, w_gu, w_down)


def _prenorm_kernel(x_ref, sh_ref, sc_ref, g_ref, o_ref):
    o_ref[...] = (_rms(x_ref[...], g_ref[...]) * (1.0 + sc_ref[...]) + sh_ref[...]).astype(o_ref.dtype)


def _prenorm(x, shift, scale, g, bm, rows_per_batch):
    m, d = x.shape
    ms = lambda a: _mod_spec(a, bm, rows_per_batch)
    return pl.pallas_call(
        _prenorm_kernel,
        grid=(m // bm,),
        in_specs=[pl.BlockSpec((bm, d), lambda i: (i, 0)), ms(shift), ms(scale), _vec_spec(d)],
        out_specs=pl.BlockSpec((bm, d), lambda i: (i, 0)),
        out_shape=jax.ShapeDtypeStruct((m, d), BF16),
        compiler_params=_cparams("parallel"),
        name="prenorm",
    )(x, shift, scale, g.reshape(1, d))


def _mm_kernel(x_ref, w_ref, o_ref):
    o_ref[...] = _dot(x_ref[...], w_ref[...]).astype(o_ref.dtype)


def _mm(x, w, out_dtype, bm, bn, name):
    m, k = x.shape
    n = w.shape[1]
    return pl.pallas_call(
        _mm_kernel,
        grid=(m // bm, n // bn),
        in_specs=[pl.BlockSpec((bm, k), lambda i, j: (i, 0)),
                  pl.BlockSpec((k, bn), lambda i, j: (0, j))],
        out_specs=pl.BlockSpec((bm, bn), lambda i, j: (i, j)),
        out_shape=jax.ShapeDtypeStruct((m, n), out_dtype),
        compiler_params=_cparams("parallel", "arbitrary"),
        name=name,
    )(x, w)


def _ssd_kernel(xbc_ref, z_ref, dtr_ref, prev_ref, h0_ref, cw_ref, cb_ref, dtb_ref, alog_ref,
                dskip_ref, gn_ref, e_ref, et_ref, y_ref, hout_ref,
                xp_scr, act_scr, y_scr, h_scr, dt_scr, *, q, t_valid, has_init):
    c = pl.program_id(1)
    tail = SUBLANES

    @pl.when(c == 0)
    def _():
        xp_scr[0:tail, :] = jnp.zeros((tail, CONV_DIM), F32)
        xp_scr[tail - (CONV_W - 1):tail, :] = prev_ref[...]
        if has_init:
            h_scr[...] = h0_ref[...]
        else:
            h_scr[...] = jnp.zeros_like(h_scr)

    tp = xbc_ref.shape[0]
    if tp < q:
        xp_scr[tail:tail + q, :] = jnp.zeros((q, CONV_DIM), F32)
        xp_scr[tail:tail + tp, :] = xbc_ref[...].astype(F32)
        dt_scr[...] = jnp.zeros_like(dt_scr)
        dt_scr[0:tp, :] = dtr_ref[...]
    else:
        xp_scr[tail:tail + q, :] = xbc_ref[...].astype(F32)
        dt_scr[...] = dtr_ref[...]

    slab = 512
    for s in range(CONV_DIM // slab):
        cs = slice(s * slab, (s + 1) * slab)
        acc = cb_ref[:, cs] + cw_ref[0:1, cs] * xp_scr[tail - 3:tail - 3 + q, cs]
        acc = acc + cw_ref[1:2, cs] * xp_scr[tail - 2:tail - 2 + q, cs]
        acc = acc + cw_ref[2:3, cs] * xp_scr[tail - 1:tail - 1 + q, cs]
        acc = acc + cw_ref[3:4, cs] * xp_scr[tail:tail + q, cs]
        act_scr[:, cs] = _silu(acc)
    xp_scr[0:tail, :] = xp_scr[q:q + tail, :]

    x = dt_scr[:, 0:SSM_HEADS] + dtb_ref[...]
    dt = jnp.maximum(x, 0.0) + jnp.log1p(jnp.exp(-jnp.abs(x)))
    if t_valid < q:
        rows = lax.broadcasted_iota(jnp.int32, (q, SSM_HEADS), 0)
        dt = jnp.where(rows < t_valid, dt, 0.0)
    a = -jnp.exp(alog_ref[...])
    la = dt * a
    ri = lax.broadcasted_iota(jnp.int32, (q, q), 0)
    ci = lax.broadcasted_iota(jnp.int32, (q, q), 1)
    causal = ri >= ci
    ltri = jnp.where(causal, 1.0, 0.0).astype(BF16)
    utri = jnp.where(ri <= ci, 1.0, 0.0).astype(BF16)
    parts = _split3(la)
    cum = _dot(ltri, parts[0]) + _dot(ltri, parts[1]) + _dot(ltri, parts[2])
    cum_t = _dot_tn(parts[0], utri) + _dot_tn(parts[1], utri) + _dot_tn(parts[2], utri)
    cum_last = cum[q - 1:q, :]
    dend = jnp.exp(cum_last - cum)
    ecum = jnp.exp(cum)

    stack = jnp.concatenate([dt, dt * dend, ecum], axis=0)
    s_hi, s_lo = _split2(stack)
    e = e_ref[...]
    exp3 = _dot(s_hi, e) + _dot(s_lo, e)
    dt_e = exp3[0:q]
    dtd_e = exp3[q:2 * q]
    ecum_e = exp3[2 * q:3 * q]
    xs = act_scr[:, 0:D_INNER]
    xdt = (xs * dt_e).astype(BF16)
    xdd = (xs * dtd_e).astype(BF16)

    cdl = jnp.broadcast_to(jnp.exp(cum_t[:, q - 1:q]), (SSM_HEADS, LANES))
    c_hi, c_lo = _split2(cdl)
    et = et_ref[...]
    cdec = _dot(et, c_hi) + _dot(et, c_lo)

    lane = lax.broadcasted_iota(jnp.int32, (q, LANES), 1)
    lo_half = lane < SSM_HEAD_DIM
    gw = SSM_HPG * SSM_HEAD_DIM
    for g in range(SSM_GROUPS):
        bg = act_scr[:, D_INNER + g * D_STATE:D_INNER + (g + 1) * D_STATE].astype(BF16)
        cg = act_scr[:, D_INNER + SSM_GROUPS * D_STATE + g * D_STATE:
                     D_INNER + SSM_GROUPS * D_STATE + (g + 1) * D_STATE].astype(BF16)
        cbm = _dot_nt(cg, bg)
        hg = h_scr[g * gw:(g + 1) * gw, :]
        yoff = _dot_nt(cg, hg.astype(BF16))
        for pr in range(SSM_HPG // 2):
            h0 = g * SSM_HPG + 2 * pr
            a0 = cbm * jnp.exp(jnp.where(causal, cum[:, h0:h0 + 1] - cum_t[h0:h0 + 1, :], NEG_INF))
            a1 = cbm * jnp.exp(jnp.where(causal, cum[:, h0 + 1:h0 + 2] - cum_t[h0 + 1:h0 + 2, :], NEG_INF))
            lhs = jnp.concatenate([a0, a1], axis=1).astype(BF16)
            ls = slice(h0 * SSM_HEAD_DIM, h0 * SSM_HEAD_DIM + LANES)
            x2 = xdt[:, ls]
            zero = jnp.zeros_like(x2)
            rhs = jnp.concatenate([jnp.where(lo_half, x2, zero), jnp.where(lo_half, zero, x2)], axis=0)
            yd = _dot(lhs, rhs)
            y_scr[:, ls] = yd + yoff[:, pr * LANES:(pr + 1) * LANES] * ecum_e[:, ls]
        st = _dot_tn(xdd[:, g * gw:(g + 1) * gw], bg)
        h_scr[g * gw:(g + 1) * gw, :] = hg * cdec[g * gw:(g + 1) * gw, :] + st

    y = y_scr[0:tp, :] + act_scr[0:tp, 0:D_INNER] * dskip_ref[...]
    y = _rms(y * _silu(z_ref[...].astype(F32)), gn_ref[...])
    y_ref[...] = y.astype(y_ref.dtype)

    @pl.when(c == pl.num_programs(1) - 1)
    def _():
        hout_ref[...] = h_scr[...]


def _ssd(xbc, z, dtr, conv_prev, h0, conv_w, conv_b, dt_bias, a_log, d_skip, g_norm):
    b, t, _ = xbc.shape
    q = SSD_Q
    t_valid = min(t, q)
    if t >= q:
        assert t % q == 0
        nc, bt = t // q, q
    else:
        bt = -(-t // SUBLANES) * SUBLANES
        padt = lambda v: jnp.pad(v.astype(F32), ((0, 0), (0, bt - t), (0, 0)))
        xbc, z, dtr = padt(xbc), padt(z), padt(dtr)
        nc = 1
    has_init = h0 is not None
    if not has_init:
        h0 = jnp.zeros((b, SUBLANES, D_STATE), F32)
    hp = SSM_HEADS * SSM_HEAD_DIM
    e = jnp.asarray(np.kron(np.eye(SSM_HEADS, dtype=np.float32), np.ones((1, SSM_HEAD_DIM), np.float32)), BF16)
    et = jnp.asarray(np.kron(np.eye(SSM_HEADS, dtype=np.float32), np.ones((SSM_HEAD_DIM, 1), np.float32)), BF16)
    dskip_e = jnp.repeat(d_skip, SSM_HEAD_DIM).reshape(1, D_INNER)
    kern = functools.partial(_ssd_kernel, q=q, t_valid=t_valid, has_init=has_init)
    h0_block = (None, hp, D_STATE) if has_init else (None, SUBLANES, D_STATE)
    y, h = pl.pallas_call(
        kern,
        grid=(b, nc),
        in_specs=[pl.BlockSpec((None, bt, CONV_DIM), lambda i, c: (i, c, 0)),
                  pl.BlockSpec((None, bt, D_INNER), lambda i, c: (i, c, 0)),
                  pl.BlockSpec((None, bt, LANES), lambda i, c: (i, c, 0)),
                  pl.BlockSpec((None, CONV_W - 1, CONV_DIM), lambda i, c: (i, 0, 0)),
                  pl.BlockSpec(h0_block, lambda i, c: (i, 0, 0)),
                  pl.BlockSpec((CONV_W, CONV_DIM), lambda i, c: (0, 0)),
                  _vec_spec(CONV_DIM), _vec_spec(SSM_HEADS), _vec_spec(SSM_HEADS),
                  _vec_spec(D_INNER), _vec_spec(D_INNER),
                  pl.BlockSpec((SSM_HEADS, D_INNER), lambda i, c: (0, 0)),
                  pl.BlockSpec((D_INNER, SSM_HEADS), lambda i, c: (0, 0))],
        out_specs=[pl.BlockSpec((None, bt, D_INNER), lambda i, c: (i, c, 0)),
                   pl.BlockSpec((None, hp, D_STATE), lambda i, c: (i, 0, 0))],
        out_shape=[jax.ShapeDtypeStruct((b, bt * nc, D_INNER), BF16),
                   jax.ShapeDtypeStruct((b, hp, D_STATE), F32)],
        scratch_shapes=[pltpu.VMEM((SUBLANES + q, CONV_DIM), F32),
                        pltpu.VMEM((q, CONV_DIM), F32),
                        pltpu.VMEM((q, D_INNER), F32),
                        pltpu.VMEM((hp, D_STATE), F32),
                        pltpu.VMEM((q, LANES), F32)],
        compiler_params=_cparams("parallel", "arbitrary"),
        name="ssd",
    )(xbc, z, dtr, conv_prev, h0, conv_w, conv_b.reshape(1, CONV_DIM), dt_bias.reshape(1, SSM_HEADS),
      a_log.reshape(1, SSM_HEADS), dskip_e, g_norm.reshape(1, D_INNER), e, et)
    return y[:, :t], h


def _attn_prompt_kernel(*refs, slopes, seq):
    qkv_refs = refs[:3 * N_DIL]
    o_ref, og_scr, lg_scr = refs[3 * N_DIL:]
    blk = ATT_BLOCK
    h = pl.program_id(1)
    qi = lax.broadcasted_iota(jnp.int32, (blk, 2 * blk), 0)
    ki = lax.broadcasted_iota(jnp.int32, (blk, 2 * blk), 1)
    dist = qi + blk - ki
    scale = ATT_HEAD_DIM ** -0.5
    for gi, (window, dil) in enumerate(DIL_GROUPS):
        q_ref, k_ref, v_ref = qkv_refs[gi], qkv_refs[N_DIL + gi], qkv_refs[2 * N_DIL + gi]
        slope = slopes[gi][0]
        for hh in range(1, ATT_HPG):
            slope = jnp.where(h == hh, slopes[gi][hh], slope)
        valid2 = (dist >= 0) & (dist <= window // dil)
        bias2 = slope * (dil * dist).astype(F32)
        nb = seq // dil // blk

        def rows(start, size):
            return pl.ds(start, size) if dil == 1 else pl.ds(start, size, stride=dil)

        def block(r, n, first):
            q0 = r + dil * n * blk
            qb = q_ref[rows(q0, blk), :].astype(BF16)
            if first:
                kb = k_ref[rows(q0, blk), :].astype(BF16)
                vb = v_ref[rows(q0, blk), :].astype(BF16)
                s = jnp.where(valid2[:, blk:], _dot_nt(qb, kb) * scale - bias2[:, blk:], NEG_INF)
            else:
                k0 = q0 - dil * blk
                kb = k_ref[rows(k0, 2 * blk), :].astype(BF16)
                vb = v_ref[rows(k0, 2 * blk), :].astype(BF16)
                s = jnp.where(valid2, _dot_nt(qb, kb) * scale - bias2, NEG_INF)
            m = jnp.max(s, axis=-1, keepdims=True)
            p = jnp.exp(s - m)
            l = jnp.sum(p, axis=-1, keepdims=True)
            og_scr[gi, rows(q0, blk), :] = _dot(p.astype(BF16), vb) / l
            lg_scr[gi, rows(q0, blk), :] = jnp.broadcast_to(m + jnp.log(l), (blk, ATT_HEAD_DIM))

        if dil == 1:
            block(0, 0, True)

            def body(n, carry):
                block(0, n, False)
                return carry
            lax.fori_loop(1, nb, body, 0)
        else:
            for r in range(dil):
                for n in range(nb):
                    block(r, n, n == 0)

    l0, l1, l2 = lg_scr[0], lg_scr[1], lg_scr[2]
    mm = jnp.maximum(jnp.maximum(l0, l1), l2)
    w0, w1, w2 = jnp.exp(l0 - mm), jnp.exp(l1 - mm), jnp.exp(l2 - mm)
    att = (w0 * og_scr[0] + w1 * og_scr[1] + w2 * og_scr[2]) / (w0 + w1 + w2)
    o_ref[...] = att.astype(o_ref.dtype)


def _attn_prompt(qkv):
    b, s, w = qkv.shape
    for window, dil in DIL_GROUPS:
        assert s % (dil * ATT_BLOCK) == 0
    hd = ATT_HEAD_DIM
    specs = [pl.BlockSpec((None, s, hd), functools.partial(lambda i, h, c: (i, 0, c + h), c=part * ATT_HEADS + gi * ATT_HPG))
             for part in range(3) for gi in range(N_DIL)]
    kern = functools.partial(_attn_prompt_kernel, seq=s,
                             slopes=[[float(v) for v in row] for row in _alibi_slopes()])
    return pl.pallas_call(
        kern,
        grid=(b, ATT_HPG),
        in_specs=specs,
        out_specs=pl.BlockSpec((None, s, hd), lambda i, h: (i, 0, h)),
        out_shape=jax.ShapeDtypeStruct((b, s, ATT_OUT_WIDTH), BF16),
        scratch_shapes=[pltpu.VMEM((N_DIL, s, hd), F32), pltpu.VMEM((N_DIL, s, hd), F32)],
        compiler_params=_cparams("parallel", "arbitrary"),
        name="attn_prompt",
    )(*([qkv] * (3 * N_DIL)))


def _attn_sample_kernel(qkv_ref, c0_ref, c1_ref, c2_ref, o_ref, *, t, slopes):
    scale = ATT_HEAD_DIM ** -0.5
    caches = (c0_ref, c1_ref, c2_ref)
    nk = ATT_BLOCK
    step = lax.broadcasted_iota(jnp.int32, (nk, ATT_HPG, 1), 0)
    hid = lax.broadcasted_iota(jnp.int32, (ATT_HPG, 1), 0)
    for tq in range(t):
        outs, lses = [], []
        for gi, (window, dil) in enumerate(DIL_GROUPS):
            kmax = window // dil
            slope = jnp.full((ATT_HPG, 1), slopes[gi][0], F32)
            for hh in range(1, ATT_HPG):
                slope = jnp.where(hid == hh, slopes[gi][hh], slope)
            q = qkv_ref[tq, gi]
            cref = caches[gi]
            if dil == 1:
                kc, vc = cref[:, 0], cref[:, 1]
                dist = kmax + tq - step
                new = [(j, tq - j) for j in range(tq + 1)]
            else:
                kc, vc = cref[:, tq, 0], cref[:, tq, 1]
                dist = kmax - step
                new = [(tq, 0)]
            s = jnp.sum(kc * q[None], axis=-1, keepdims=True) * scale - slope[None] * (dil * dist).astype(F32)
            s = jnp.where(dist <= kmax, s, NEG_INF)
            s_new = [jnp.sum(qkv_ref[j, N_DIL + gi] * q, axis=-1, keepdims=True) * scale - slope * float(dil * d)
                     for j, d in new]
            m = jnp.max(s, axis=0)
            for sn in s_new:
                m = jnp.maximum(m, sn)
            p = jnp.exp(s - m[None])
            l = jnp.sum(p, axis=0)
            o = jnp.sum(p * vc, axis=0)
            for (j, _), sn in zip(new, s_new):
                pn = jnp.exp(sn - m)
                l = l + pn
                o = o + pn * qkv_ref[j, 2 * N_DIL + gi]
            outs.append(o / l)
            lses.append(m + jnp.log(l))
        mm = jnp.maximum(jnp.maximum(lses[0], lses[1]), lses[2])
        ws = [jnp.exp(x - mm) for x in lses]
        o_ref[tq] = (ws[0] * outs[0] + ws[1] * outs[1] + ws[2] * outs[2]) / (ws[0] + ws[1] + ws[2])


def _attn_sample(qkv, caches):
    b, t, w = qkv.shape
    qkv5 = qkv.reshape(b, t, 3 * N_DIL, ATT_HPG, ATT_HEAD_DIM)
    specs = [pl.BlockSpec((None, t, 3 * N_DIL, ATT_HPG, ATT_HEAD_DIM), lambda i: (i, 0, 0, 0, 0))]
    views = []
    for (window, dil), cache in zip(DIL_GROUPS, caches):
        lb = cache.shape[1]
        assert lb == window and lb // dil == ATT_BLOCK and (dil == 1 or dil >= t)
        if dil == 1:
            views.append(cache)
            specs.append(pl.BlockSpec((None, lb, 2, ATT_HPG, ATT_HEAD_DIM), lambda i: (i, 0, 0, 0, 0)))
        else:
            views.append(cache.reshape(b, lb // dil, dil, 2, ATT_HPG, ATT_HEAD_DIM))
            specs.append(pl.BlockSpec((None, lb // dil, t, 2, ATT_HPG, ATT_HEAD_DIM), lambda i: (i, 0, 0, 0, 0, 0)))
    kern = functools.partial(_attn_sample_kernel, t=t,
                             slopes=[[float(v) for v in row] for row in _alibi_slopes()])
    att = pl.pallas_call(
        kern,
        grid=(b,),
        in_specs=specs,
        out_specs=pl.BlockSpec((None, t, ATT_HPG, ATT_HEAD_DIM), lambda i: (i, 0, 0, 0)),
        out_shape=jax.ShapeDtypeStruct((b, t, ATT_HPG, ATT_HEAD_DIM), F32),
        compiler_params=_cparams("parallel"),
        name="attn_sample",
    )(qkv5, *views)
    return att.reshape(b, t, ATT_OUT_WIDTH)


def _merge_kernel(att_ref, y_ref, gs_ref, ga_ref, ws_ref, wa_ref, o_ref):
    bs = _dot(y_ref[...], ws_ref[...])
    ba = _dot(att_ref[...].astype(BF16), wa_ref[...])
    o_ref[...] = (_sigmoid(gs_ref[...].astype(F32)) * bs + _sigmoid(ga_ref[...].astype(F32)) * ba).astype(o_ref.dtype)


def _merge(att, y, gates, w_ssm, w_att, bm):
    m = y.shape[0]
    bn = 512
    nj = D_MODEL // bn
    return pl.pallas_call(
        _merge_kernel,
        grid=(m // bm, nj),
        in_specs=[pl.BlockSpec((bm, ATT_OUT_WIDTH), lambda i, j: (i, 0)),
                  pl.BlockSpec((bm, D_INNER), lambda i, j: (i, 0)),
                  pl.BlockSpec((bm, bn), lambda i, j: (i, j)),
                  pl.BlockSpec((bm, bn), lambda i, j: (i, j + nj)),
                  pl.BlockSpec((D_INNER, bn), lambda i, j: (0, j)),
                  pl.BlockSpec((ATT_OUT_WIDTH, bn), lambda i, j: (0, j))],
        out_specs=pl.BlockSpec((bm, bn), lambda i, j: (i, j)),
        out_shape=jax.ShapeDtypeStruct((m, D_MODEL), BF16),
        compiler_params=_cparams("parallel", "arbitrary"),
        name="merge",
    )(att, y, gates, gates, w_ssm, w_att)


def _outproj_kernel(mg_ref, x_ref, gt_ref, gpost_ref, w_ref, o_ref):
    out = _dot(mg_ref[...], w_ref[...])
    o_ref[...] = x_ref[...] + gt_ref[...] * _rms(out, gpost_ref[...])


def _outproj(merged, x, gate, g_post, w_out, bm, rows_per_batch):
    m, d = x.shape
    return pl.pallas_call(
        _outproj_kernel,
        grid=(m // bm,),
        in_specs=[pl.BlockSpec((bm, d), lambda i: (i, 0)),
                  pl.BlockSpec((bm, d), lambda i: (i, 0)),
                  _mod_spec(gate, bm, rows_per_batch), _vec_spec(d),
                  pl.BlockSpec((d, d), lambda i: (0, 0))],
        out_specs=pl.BlockSpec((bm, d), lambda i: (i, 0)),
        out_shape=jax.ShapeDtypeStruct((m, d), F32),
        compiler_params=_cparams("parallel"),
        name="outproj",
    )(merged, x, gate, g_post.reshape(1, d), w_out)


def _layer(x3, mod, conv_prev, ssm_prev, kv_bufs, p, bm):
    b, t, d = x3.shape
    m = b * t
    prompt = kv_bufs is None
    x = x3.reshape(m, d)
    if prompt:
        mods = [mod[:, k].reshape(b, 1, d) for k in range(3 * N_SUB)]
    else:
        mods = [jnp.repeat(mod[:, k], t, axis=0) for k in range(3 * N_SUB)]

    x = _ffn(x, mods[0], mods[1], mods[2], p['g_pre_ffn1'], p['g_post_ffn1'], p['w_gu_ffn1'],
             p['w_down_ffn1'], bm, t)

    h = _prenorm(x, mods[3], mods[4], p['g_pre_mix'], bm, t)
    bn = 512
    z = _mm(h, p['w_z'], BF16, bm, bn, "inproj_z")
    xbc = _mm(h, p['w_xbc'], BF16, bm, bn, "inproj_xbc")
    dtr = _mm(h, p['w_dt'], F32, bm, LANES, "inproj_dt")
    qkv = _mm(h, p['w_qkv'], F32, bm, bn, "inproj_qkv")
    gates = _mm(h, p['w_gates'], BF16, bm, bn, "inproj_gates")

    xbc3 = xbc.reshape(b, t, CONV_DIM)
    if prompt:
        conv_prev = jnp.zeros((b, CONV_W - 1, CONV_DIM), F32)
        conv_new = xbc3[:, t - (CONV_W - 1):].astype(F32)
    else:
        conv_new = jnp.concatenate([conv_prev, xbc3.astype(F32)], axis=1)[:, -(CONV_W - 1):]
    h0 = None if prompt else ssm_prev.reshape(b, SSM_HEADS * SSM_HEAD_DIM, D_STATE)
    y, ssm_new = _ssd(xbc3, z.reshape(b, t, D_INNER), dtr.reshape(b, t, LANES), conv_prev, h0,
                      p['conv_w'], p['conv_b'], p['dt_bias'], p['a_log'], p['d_skip'], p['g_ssm_norm'])
    ssm_new = ssm_new.reshape(b, SSM_HEADS, SSM_HEAD_DIM, D_STATE)

    qkv3 = qkv.reshape(b, t, 3 * ATT_QKV_WIDTH)
    kv_new = []
    for gi, (window, dil) in enumerate(DIL_GROUPS):
        c0 = ATT_QKV_WIDTH + gi * ATT_OUT_WIDTH
        rows = slice(t - min(window, t), t) if prompt else slice(0, t)
        kg = qkv3[:, rows, c0:c0 + ATT_OUT_WIDTH].reshape(b, -1, ATT_HPG, ATT_HEAD_DIM)
        vg = qkv3[:, rows, ATT_QKV_WIDTH + c0:ATT_QKV_WIDTH + c0 + ATT_OUT_WIDTH].reshape(
            b, -1, ATT_HPG, ATT_HEAD_DIM)
        new = jnp.stack([kg, vg], axis=2)
        if prompt:
            kv_new.append(new)
        else:
            lb = kv_bufs[gi].shape[1]
            kv_new.append(jnp.concatenate([kv_bufs[gi], new], axis=1)[:, -lb:])
    if prompt:
        att = _attn_prompt(qkv3)
    else:
        att = _attn_sample(qkv3, kv_bufs)

    merged = _merge(att.reshape(m, ATT_OUT_WIDTH), y.reshape(m, D_INNER), gates, p['w_ssm_proj'],
                    p['w_att_proj'], bm)
    x = _outproj(merged, x, mods[5], p['g_post_mix'], p['w_out'], bm, t)

    x = _ffn(x, mods[6], mods[7], mods[8], p['g_pre_ffn2'], p['g_post_ffn2'], p['w_gu_ffn2'],
             p['w_down_ffn2'], bm, t)
    return x.reshape(b, t, d), conv_new, ssm_new, kv_new


def _prep_weights(w_gu_ffn1, w_down_ffn1, w_in, w_ssm_proj, w_att_proj, w_out, w_gu_ffn2, w_down_ffn2):
    o = 0
    segs = {}
    for name, size in (('z', D_INNER), ('xbc', CONV_DIM), ('dt', SSM_HEADS), ('qkv', 3 * ATT_QKV_WIDTH),
                       ('gates', 2 * D_MODEL)):
        segs[name] = w_in[:, o:o + size].astype(BF16)
        o += size
    segs['dt'] = jnp.pad(segs['dt'], ((0, 0), (0, LANES - SSM_HEADS)))
    return {
        'w_gu_ffn1': w_gu_ffn1.astype(BF16), 'w_down_ffn1': w_down_ffn1.astype(BF16),
        'w_gu_ffn2': w_gu_ffn2.astype(BF16), 'w_down_ffn2': w_down_ffn2.astype(BF16),
        'w_z': segs['z'], 'w_xbc': segs['xbc'], 'w_dt': segs['dt'], 'w_qkv': segs['qkv'],
        'w_gates': segs['gates'],
        'w_ssm_proj': w_ssm_proj.astype(BF16), 'w_att_proj': w_att_proj.astype(BF16),
        'w_out': w_out.astype(BF16),
    }


def kernel(x_prompt, x_sample, c_prompt, c_sample, state_ssm, state_conv, cache_kv_w128, cache_kv_w512, cache_kv_w2048, w_ada, b_ada, g_pre_ffn1, g_post_ffn1, w_gu_ffn1, w_down_ffn1, g_pre_mix, g_post_mix, w_in, conv_w, conv_b, dt_bias, a_log, d_skip, g_ssm_norm, w_ssm_proj, w_att_proj, w_out, g_pre_ffn2, g_post_ffn2, w_gu_ffn2, w_down_ffn2):
    depth = w_ada.shape[0]
    kv_in = (cache_kv_w128, cache_kv_w512, cache_kv_w2048)
    bp = x_prompt.shape[0]
    yp, ys = x_prompt, x_sample
    outs = [[] for _ in range(10)]
    for l in range(depth):
        p = _prep_weights(w_gu_ffn1[l], w_down_ffn1[l], w_in[l], w_ssm_proj[l], w_att_proj[l], w_out[l],
                          w_gu_ffn2[l], w_down_ffn2[l])
        p.update({'g_pre_ffn1': g_pre_ffn1[l], 'g_post_ffn1': g_post_ffn1[l], 'g_pre_mix': g_pre_mix[l],
                  'g_post_mix': g_post_mix[l], 'g_pre_ffn2': g_pre_ffn2[l], 'g_post_ffn2': g_post_ffn2[l],
                  'conv_w': conv_w[l], 'conv_b': conv_b[l], 'dt_bias': dt_bias[l], 'a_log': a_log[l],
                  'd_skip': d_skip[l], 'g_ssm_norm': g_ssm_norm[l]})
        c_all = jnp.concatenate([c_prompt, c_sample], axis=0)
        mod = _ada(c_all, w_ada[l], b_ada[l]).reshape(c_all.shape[0], 3 * N_SUB, D_MODEL)
        yp, cp, sp, kp = _layer(yp, mod[:bp], None, None, None, p, 512)
        ys, cs, ss, ksm = _layer(ys, mod[bp:], state_conv[l], state_ssm[l], tuple(kv[l] for kv in kv_in), p,
                                 ys.shape[0] * ys.shape[1])
        for lst, val in zip(outs, (sp, ss, cp, cs, kp[0], ksm[0], kp[1], ksm[1], kp[2], ksm[2])):
            lst.append(val)
    return (yp, ys) + tuple(jnp.stack(o) for o in outs)
```

```python
import functools
import math

import numpy as np
import jax
import jax.numpy as jnp
from jax import lax
from jax.experimental import pallas as pl
from jax.experimental.pallas import tpu as pltpu

D_MODEL = 2048
SSM_HEAD_DIM = 64
D_INNER = 2 * D_MODEL
SSM_HEADS = D_INNER // SSM_HEAD_DIM
SSM_GROUPS = 8
SSM_HPG = SSM_HEADS // SSM_GROUPS
D_STATE = 128
CONV_W = 4
CONV_DIM = D_INNER + 2 * SSM_GROUPS * D_STATE
ATT_HEAD_DIM = 128
DIL_GROUPS = ((128, 1), (512, 4), (2048, 16))
N_DIL = len(DIL_GROUPS)
ATT_HPG = 4
ATT_HEADS = N_DIL * ATT_HPG
ATT_QKV_WIDTH = ATT_HEADS * ATT_HEAD_DIM
ATT_OUT_WIDTH = ATT_HPG * ATT_HEAD_DIM
ATT_BLOCK = 128
ALIBI_MAX_BIAS = 8.0
D_FF = ((8 * D_MODEL // 3 + 255) // 256) * 256
N_SUB = 3
EPS = 1e-6
NEG_INF = -1e30

LANES = 128
SUBLANES = 8
VMEM_LIMIT = 56 * 1024 * 1024
SSD_Q = 128
MM_BM = 1024
MM_BN = 1024

F32 = jnp.float32
BF16 = jnp.bfloat16


def _cparams(*sem):
    return pltpu.CompilerParams(dimension_semantics=sem, vmem_limit_bytes=VMEM_LIMIT)


def _alibi_slopes():
    i = np.arange(1, ATT_HEADS + 1, dtype=np.float32)
    return np.exp2(-ALIBI_MAX_BIAS * i / ATT_HEADS).astype(np.float32).reshape(N_DIL, ATT_HPG)


def _silu(x):
    return x * (1.0 / (1.0 + jnp.exp(-x)))


def _sigmoid(x):
    return 1.0 / (1.0 + jnp.exp(-x))


def _rms(x, g):
    return x * lax.rsqrt(jnp.mean(x * x, axis=-1, keepdims=True) + EPS) * g


def _split2(x):
    hi = x.astype(BF16)
    lo = (x - hi.astype(F32)).astype(BF16)
    return hi, lo


def _split3(x):
    hi = x.astype(BF16)
    r = x - hi.astype(F32)
    mid = r.astype(BF16)
    lo = (r - mid.astype(F32)).astype(BF16)
    return hi, mid, lo


def _dot(a, b):
    return jnp.dot(a, b, preferred_element_type=F32)


def _dot_nt(a, b):
    return lax.dot_general(a, b, (((1,), (1,)), ((), ())), preferred_element_type=F32)


def _dot_tn(a, b):
    return lax.dot_general(a, b, (((0,), (0,)), ((), ())), preferred_element_type=F32)


def _mod_spec(mod, bm, rows_per_batch):
    if mod.ndim == 3:
        bpb = rows_per_batch // bm
        return pl.BlockSpec((None, 1, D_MODEL), lambda i, *_: (i // bpb, 0, 0))
    return pl.BlockSpec((bm, D_MODEL), lambda i, *_: (i, 0))


def _vec_spec(n):
    return pl.BlockSpec((1, n), lambda *_: (0, 0))


def _ada_kernel(c_ref, w_ref, b_ref, o_ref):
    c = _silu(c_ref[...]).astype(BF16)
    o_ref[...] = _dot(c, w_ref[...].astype(BF16)) + b_ref[...]


def _ada(c, w, b):
    m, k = c.shape
    n = w.shape[1]
    bn = 1024
    return pl.pallas_call(
        _ada_kernel,
        grid=(n // bn,),
        in_specs=[pl.BlockSpec((m, k), lambda j: (0, 0)),
                  pl.BlockSpec((k, bn), lambda j: (0, j)),
                  pl.BlockSpec((1, bn), lambda j: (0, j))],
        out_specs=pl.BlockSpec((m, bn), lambda j: (0, j)),
        out_shape=jax.ShapeDtypeStruct((m, n), F32),
        compiler_params=_cparams("parallel"),
        name="ada",
    )(c, w, b.reshape(1, n))


def _ffn_kernel(x_ref, sh_ref, sc_ref, gt_ref, gpre_ref, gpost_ref, wg_ref, wu_ref, wd_ref, o_ref,
                h_scr, acc_scr):
    j = pl.program_id(1)

    @pl.when(j == 0)
    def _():
        x = x_ref[...]
        h = _rms(x, gpre_ref[...]) * (1.0 + sc_ref[...]) + sh_ref[...]
        h_scr[...] = h.astype(BF16)
        acc_scr[...] = jnp.zeros_like(acc_scr)

    h = h_scr[...]
    g = _dot(h, wg_ref[...])
    u = _dot(h, wu_ref[...])
    a = (_silu(g) * u).astype(BF16)
    acc_scr[...] += _dot(a, wd_ref[...])

    @pl.when(j == pl.num_programs(1) - 1)
    def _():
        o_ref[...] = x_ref[...] + 0.5 * gt_ref[...] * _rms(acc_scr[...], gpost_ref[...])


def _ffn(x, shift, scale, gate, g_pre, g_post, w_gu, w_down, bm, rows_per_batch):
    m, d = x.shape
    ff = w_down.shape[0]
    bf = 512
    nj = ff // bf
    ms = lambda a: _mod_spec(a, bm, rows_per_batch)
    return pl.pallas_call(
        _ffn_kernel,
        grid=(m // bm, nj),
        in_specs=[pl.BlockSpec((bm, d), lambda i, j: (i, 0)),
                  ms(shift), ms(scale), ms(gate), _vec_spec(d), _vec_spec(d),
                  pl.BlockSpec((d, bf), lambda i, j: (0, j)),
                  pl.BlockSpec((d, bf), lambda i, j: (0, j + nj)),
                  pl.BlockSpec((bf, d), lambda i, j: (j, 0))],
        out_specs=pl.BlockSpec((bm, d), lambda i, j: (i, 0)),
        out_shape=jax.ShapeDtypeStruct((m, d), F32),
        scratch_shapes=[pltpu.VMEM((bm, d), BF16), pltpu.VMEM((bm, d), F32)],
        compiler_params=_cparams("parallel", "arbitrary"),
        name="ffn",
    )(x, shift, scale, gate, g_pre.reshape(1, d), g_post.reshape(1, d), w_gu, w_gu, w_down)


def _prenorm_kernel(x_ref, sh_ref, sc_ref, g_ref, o_ref):
    o_ref[...] = (_rms(x_ref[...], g_ref[...]) * (1.0 + sc_ref[...]) + sh_ref[...]).astype(o_ref.dtype)


def _prenorm(x, shift, scale, g, bm, rows_per_batch):
    m, d = x.shape
    ms = lambda a: _mod_spec(a, bm, rows_per_batch)
    return pl.pallas_call(
        _prenorm_kernel,
        grid=(m // bm,),
        in_specs=[pl.BlockSpec((bm, d), lambda i: (i, 0)), ms(shift), ms(scale), _vec_spec(d)],
        out_specs=pl.BlockSpec((bm, d), lambda i: (i, 0)),
        out_shape=jax.ShapeDtypeStruct((m, d), BF16),
        compiler_params=_cparams("parallel"),
        name="prenorm",
    )(x, shift, scale, g.reshape(1, d))


def _mm_kernel(x_ref, w_ref, o_ref):
    o_ref[...] = _dot(x_ref[...], w_ref[...]).astype(o_ref.dtype)


def _mm(x, w, out_dtype, bm, bn, name):
    m, k = x.shape
    n = w.shape[1]
    return pl.pallas_call(
        _mm_kernel,
        grid=(m // bm, n // bn),
        in_specs=[pl.BlockSpec((bm, k), lambda i, j: (i, 0)),
                  pl.BlockSpec((k, bn), lambda i, j: (0, j))],
        out_specs=pl.BlockSpec((bm, bn), lambda i, j: (i, j)),
        out_shape=jax.ShapeDtypeStruct((m, n), out_dtype),
        compiler_params=_cparams("parallel", "arbitrary"),
        name=name,
    )(x, w)


def _ssd_kernel(xbc_ref, z_ref, dtr_ref, prev_ref, h0_ref, cw_ref, cb_ref, dtb_ref, alog_ref,
                dskip_ref, gn_ref, e_ref, et_ref, y_ref, hout_ref,
                xp_scr, act_scr, y_scr, h_scr, dt_scr, *, q, t_valid, has_init):
    c = pl.program_id(1)
    w8 = SUBLANES
    qo = xbc_ref.shape[0]

    @pl.when(c == 0)
    def _():
        xp_scr[0:w8, :] = jnp.zeros((w8, CONV_DIM), F32)
        xp_scr[w8 - (CONV_W - 1):w8, :] = prev_ref[...]
        if has_init:
            h_scr[...] = h0_ref[...]
        else:
            h_scr[...] = jnp.zeros_like(h_scr)

    slab = 512
    if qo < q:
        assert qo == w8
        act_scr[...] = jnp.zeros_like(act_scr)
        dt_scr[...] = jnp.zeros_like(dt_scr)
        dt_scr[0:qo, :] = dtr_ref[...]
    else:
        dt_scr[...] = dtr_ref[...]
        ri = lax.broadcasted_iota(jnp.int32, (q, q), 0)
        ci = lax.broadcasted_iota(jnp.int32, (q, q), 1)
        shift = jnp.concatenate([jnp.where(ri - ci == CONV_W - 1 - i, 1.0, 0.0) for i in range(CONV_W - 1)],
                                axis=0).astype(BF16)
        for s in range(CONV_DIM // slab):
            cs = slice(s * slab, (s + 1) * slab)
            xb = xbc_ref[:, cs]
            sh = _dot(shift, xb)
            acc = cb_ref[:, cs] + cw_ref[CONV_W - 1:CONV_W, cs] * xb.astype(F32)
            for i in range(CONV_W - 1):
                acc = acc + cw_ref[i:i + 1, cs] * sh[i * q:(i + 1) * q]
            act_scr[:, cs] = _silu(acc)
    def rows8(start):
        if xbc_ref.dtype == F32:
            return xbc_ref[start:start + w8, :]
        base = start // (2 * w8) * (2 * w8)
        return xbc_ref[base:base + 2 * w8, :].astype(F32)[start - base:start - base + w8]

    xp_scr[w8:2 * w8, :] = rows8(0)
    for s in range(CONV_DIM // slab):
        cs = slice(s * slab, (s + 1) * slab)
        acc = cb_ref[:, cs]
        for i in range(CONV_W):
            off = w8 - (CONV_W - 1) + i
            acc = acc + cw_ref[i:i + 1, cs] * xp_scr[off:off + w8, cs]
        act_scr[0:w8, cs] = _silu(acc)
    xp_scr[0:w8, :] = rows8(qo - w8)

    x = dt_scr[:, 0:SSM_HEADS] + dtb_ref[...]
    dt = jnp.maximum(x, 0.0) + jnp.log1p(jnp.exp(-jnp.abs(x)))
    if t_valid < q:
        rows = lax.broadcasted_iota(jnp.int32, (q, SSM_HEADS), 0)
        dt = jnp.where(rows < t_valid, dt, 0.0)
    a = -jnp.exp(alog_ref[...])
    la = dt * a
    ri = lax.broadcasted_iota(jnp.int32, (q, q), 0)
    ci = lax.broadcasted_iota(jnp.int32, (q, q), 1)
    causal = ri >= ci
    ltri = jnp.where(causal, 1.0, 0.0).astype(BF16)
    utri = jnp.where(ri <= ci, 1.0, 0.0).astype(BF16)
    parts = _split3(la)
    cum = _dot(ltri, parts[0]) + _dot(ltri, parts[1]) + _dot(ltri, parts[2])
    cum_t = _dot_tn(parts[0], utri) + _dot_tn(parts[1], utri) + _dot_tn(parts[2], utri)
    cum_last = cum[q - 1:q, :]
    dend = jnp.exp(cum_last - cum)
    ecum = jnp.exp(cum)

    stack = jnp.concatenate([dt, dt * dend, ecum], axis=0)
    s_hi, s_lo = _split2(stack)
    e = e_ref[...]
    exp3 = _dot(s_hi, e) + _dot(s_lo, e)
    dt_e = exp3[0:q]
    dtd_e = exp3[q:2 * q]
    ecum_e = exp3[2 * q:2 * q + qo]
    xs = act_scr[:, 0:D_INNER]
    xdt = (xs * dt_e).astype(BF16)
    xdd = (xs * dtd_e).astype(BF16)

    cdl = jnp.broadcast_to(jnp.exp(cum_t[:, q - 1:q]), (SSM_HEADS, LANES))
    c_hi, c_lo = _split2(cdl)
    et = et_ref[...]
    cdec = _dot(et, c_hi) + _dot(et, c_lo)

    lane = lax.broadcasted_iota(jnp.int32, (q, LANES), 1)
    lo_half = lane < SSM_HEAD_DIM
    causal_o = causal[0:qo]
    cum_o = cum[0:qo]
    gw = SSM_HPG * SSM_HEAD_DIM
    for g in range(SSM_GROUPS):
        bg = act_scr[:, D_INNER + g * D_STATE:D_INNER + (g + 1) * D_STATE].astype(BF16)
        cg = act_scr[:, D_INNER + SSM_GROUPS * D_STATE + g * D_STATE:
                     D_INNER + SSM_GROUPS * D_STATE + (g + 1) * D_STATE].astype(BF16)
        cbm = _dot_nt(cg[0:qo], bg)
        hg = h_scr[g * gw:(g + 1) * gw, :]
        yoff = _dot_nt(cg[0:qo], hg.astype(BF16))
        for pr in range(SSM_HPG // 2):
            h0 = g * SSM_HPG + 2 * pr
            a0 = cbm * jnp.exp(jnp.where(causal_o, cum_o[:, h0:h0 + 1] - cum_t[h0:h0 + 1, :], NEG_INF))
            a1 = cbm * jnp.exp(jnp.where(causal_o, cum_o[:, h0 + 1:h0 + 2] - cum_t[h0 + 1:h0 + 2, :], NEG_INF))
            lhs = jnp.concatenate([a0, a1], axis=1).astype(BF16)
            ls = slice(h0 * SSM_HEAD_DIM, h0 * SSM_HEAD_DIM + LANES)
            x2 = xdt[:, ls]
            zero = jnp.zeros_like(x2)
            rhs = jnp.concatenate([jnp.where(lo_half, x2, zero), jnp.where(lo_half, zero, x2)], axis=0)
            yd = _dot(lhs, rhs)
            y_scr[:, ls] = yd + yoff[:, pr * LANES:(pr + 1) * LANES] * ecum_e[:, ls]
        st = _dot_tn(xdd[:, g * gw:(g + 1) * gw], bg)
        h_scr[g * gw:(g + 1) * gw, :] = hg * cdec[g * gw:(g + 1) * gw, :] + st

    y = y_scr[...] + act_scr[0:qo, 0:D_INNER] * dskip_ref[...]
    y = _rms(y * _silu(z_ref[...].astype(F32)), gn_ref[...])
    y_ref[...] = y.astype(y_ref.dtype)

    @pl.when(c == pl.num_programs(1) - 1)
    def _():
        hout_ref[...] = h_scr[...]


def _ssd(xbc, z, dtr, conv_prev, h0, conv_w, conv_b, dt_bias, a_log, d_skip, g_norm):
    b, t, _ = xbc.shape
    q = SSD_Q
    t_valid = min(t, q)
    if t >= q:
        assert t % q == 0
        nc, bt = t // q, q
    else:
        bt = -(-t // SUBLANES) * SUBLANES
        padt = lambda v: jnp.pad(v.astype(F32), ((0, 0), (0, bt - t), (0, 0)))
        xbc, z, dtr = padt(xbc), padt(z), padt(dtr)
        nc = 1
    has_init = h0 is not None
    if not has_init:
        h0 = jnp.zeros((b, SUBLANES, D_STATE), F32)
    hp = SSM_HEADS * SSM_HEAD_DIM
    e = jnp.asarray(np.kron(np.eye(SSM_HEADS, dtype=np.float32), np.ones((1, SSM_HEAD_DIM), np.float32)), BF16)
    et = jnp.asarray(np.kron(np.eye(SSM_HEADS, dtype=np.float32), np.ones((SSM_HEAD_DIM, 1), np.float32)), BF16)
    dskip_e = jnp.repeat(d_skip, SSM_HEAD_DIM).reshape(1, D_INNER)
    kern = functools.partial(_ssd_kernel, q=q, t_valid=t_valid, has_init=has_init)
    h0_block = (None, hp, D_STATE) if has_init else (None, SUBLANES, D_STATE)
    y, h = pl.pallas_call(
        kern,
        grid=(b, nc),
        in_specs=[pl.BlockSpec((None, bt, CONV_DIM), lambda i, c: (i, c, 0)),
                  pl.BlockSpec((None, bt, D_INNER), lambda i, c: (i, c, 0)),
                  pl.BlockSpec((None, bt, LANES), lambda i, c: (i, c, 0)),
                  pl.BlockSpec((None, CONV_W - 1, CONV_DIM), lambda i, c: (i, 0, 0)),
                  pl.BlockSpec(h0_block, lambda i, c: (i, 0, 0)),
                  pl.BlockSpec((CONV_W, CONV_DIM), lambda i, c: (0, 0)),
                  _vec_spec(CONV_DIM), _vec_spec(SSM_HEADS), _vec_spec(SSM_HEADS),
                  _vec_spec(D_INNER), _vec_spec(D_INNER),
                  pl.BlockSpec((SSM_HEADS, D_INNER), lambda i, c: (0, 0)),
                  pl.BlockSpec((D_INNER, SSM_HEADS), lambda i, c: (0, 0))],
        out_specs=[pl.BlockSpec((None, bt, D_INNER), lambda i, c: (i, c, 0)),
                   pl.BlockSpec((None, hp, D_STATE), lambda i, c: (i, 0, 0))],
        out_shape=[jax.ShapeDtypeStruct((b, bt * nc, D_INNER), BF16),
                   jax.ShapeDtypeStruct((b, hp, D_STATE), F32)],
        scratch_shapes=[pltpu.VMEM((2 * SUBLANES, CONV_DIM), F32),
                        pltpu.VMEM((q, CONV_DIM), F32),
                        pltpu.VMEM((bt, D_INNER), F32),
                        pltpu.VMEM((hp, D_STATE), F32),
                        pltpu.VMEM((q, LANES), F32)],
        compiler_params=_cparams("parallel", "arbitrary"),
        name="ssd",
    )(xbc, z, dtr, conv_prev, h0, conv_w, conv_b.reshape(1, CONV_DIM), dt_bias.reshape(1, SSM_HEADS),
      a_log.reshape(1, SSM_HEADS), dskip_e, g_norm.reshape(1, D_INNER), e, et)
    return y[:, :t], h


def _attn_prompt_kernel(*refs, slopes, seq):
    qkv_refs = refs[:3 * N_DIL]
    o_ref, og_scr, lg_scr = refs[3 * N_DIL:]
    blk = ATT_BLOCK
    h = pl.program_id(1)
    qi = lax.broadcasted_iota(jnp.int32, (blk, 2 * blk), 0)
    ki = lax.broadcasted_iota(jnp.int32, (blk, 2 * blk), 1)
    dist = qi + blk - ki
    scale = ATT_HEAD_DIM ** -0.5
    for gi, (window, dil) in enumerate(DIL_GROUPS):
        q_ref, k_ref, v_ref = qkv_refs[gi], qkv_refs[N_DIL + gi], qkv_refs[2 * N_DIL + gi]
        slope = slopes[gi][0]
        for hh in range(1, ATT_HPG):
            slope = jnp.where(h == hh, slopes[gi][hh], slope)
        valid2 = (dist >= 0) & (dist <= window // dil)
        bias_any = jnp.where(valid2, -slope * (dil * dist).astype(F32), NEG_INF)
        bias_first = jnp.where(ki >= blk, bias_any, NEG_INF)
        nb = seq // dil // blk

        def rows(start, size):
            return pl.ds(start, size) if dil == 1 else pl.ds(start, size, stride=dil)

        starts = [r + dil * n * blk for r in range(dil) for n in range(nb)]
        firsts = [n == 0 for r in range(dil) for n in range(nb)]

        def band(ref, q0, first):
            if first:
                cur = ref[rows(q0, blk), :].astype(BF16)
                return jnp.concatenate([cur, cur], axis=0)
            return ref[rows(q0 - dil * blk, 2 * blk), :].astype(BF16)

        qs = jnp.stack([q_ref[rows(q0, blk), :].astype(BF16) for q0 in starts])
        ks = jnp.stack([band(k_ref, q0, f) for q0, f in zip(starts, firsts)])
        vs = jnp.stack([band(v_ref, q0, f) for q0, f in zip(starts, firsts)])
        bias = jnp.stack([bias_first if f else bias_any for f in firsts])
        s = jnp.einsum('bqd,bkd->bqk', qs, ks, preferred_element_type=F32) * scale + bias
        m = jnp.max(s, axis=-1, keepdims=True)
        p = jnp.exp(s - m)
        l = jnp.sum(p, axis=-1, keepdims=True)
        o = jnp.einsum('bqk,bkd->bqd', p.astype(BF16), vs, preferred_element_type=F32) / l
        lse = m + jnp.log(l)
        for idx, q0 in enumerate(starts):
            og_scr[gi, rows(q0, blk), :] = o[idx]
            lg_scr[gi, rows(q0, blk), :] = jnp.broadcast_to(lse[idx], (blk, ATT_HEAD_DIM))

    l0, l1, l2 = lg_scr[0], lg_scr[1], lg_scr[2]
    mm = jnp.maximum(jnp.maximum(l0, l1), l2)
    w0, w1, w2 = jnp.exp(l0 - mm), jnp.exp(l1 - mm), jnp.exp(l2 - mm)
    att = (w0 * og_scr[0] + w1 * og_scr[1] + w2 * og_scr[2]) / (w0 + w1 + w2)
    o_ref[...] = att.astype(o_ref.dtype)


def _attn_prompt(qkv):
    b, s, w = qkv.shape
    for window, dil in DIL_GROUPS:
        assert s % (dil * ATT_BLOCK) == 0
    hd = ATT_HEAD_DIM
    specs = [pl.BlockSpec((None, s, hd), functools.partial(lambda i, h, c: (i, 0, c + h), c=part * ATT_HEADS + gi * ATT_HPG))
             for part in range(3) for gi in range(N_DIL)]
    kern = functools.partial(_attn_prompt_kernel, seq=s,
                             slopes=[[float(v) for v in row] for row in _alibi_slopes()])
    return pl.pallas_call(
        kern,
        grid=(b, ATT_HPG),
        in_specs=specs,
        out_specs=pl.BlockSpec((None, s, hd), lambda i, h: (i, 0, h)),
        out_shape=jax.ShapeDtypeStruct((b, s, ATT_OUT_WIDTH), BF16),
        scratch_shapes=[pltpu.VMEM((N_DIL, s, hd), F32), pltpu.VMEM((N_DIL, s, hd), F32)],
        compiler_params=_cparams("parallel", "arbitrary"),
        name="attn_prompt",
    )(*([qkv] * (3 * N_DIL)))


def _attn_sample_kernel(qkv_ref, c0_ref, c1_ref, c2_ref, o_ref, *, t, slopes):
    scale = ATT_HEAD_DIM ** -0.5
    caches = (c0_ref, c1_ref, c2_ref)
    nk = ATT_BLOCK
    step = lax.broadcasted_iota(jnp.int32, (nk, ATT_HPG, 1), 0)
    hid = lax.broadcasted_iota(jnp.int32, (ATT_HPG, 1), 0)
    for tq in range(t):
        outs, lses = [], []
        for gi, (window, dil) in enumerate(DIL_GROUPS):
            kmax = window // dil
            slope = jnp.full((ATT_HPG, 1), slopes[gi][0], F32)
            for hh in range(1, ATT_HPG):
                slope = jnp.where(hid == hh, slopes[gi][hh], slope)
            q = qkv_ref[tq, gi]
            cref = caches[gi]
            if dil == 1:
                kc, vc = cref[:, 0], cref[:, 1]
                dist = kmax + tq - step
                new = [(j, tq - j) for j in range(tq + 1)]
            else:
                kc, vc = cref[:, tq, 0], cref[:, tq, 1]
                dist = kmax - step
                new = [(tq, 0)]
            s = jnp.sum(kc * q[None], axis=-1, keepdims=True) * scale - slope[None] * (dil * dist).astype(F32)
            s = jnp.where(dist <= kmax, s, NEG_INF)
            s_new = [jnp.sum(qkv_ref[j, N_DIL + gi] * q, axis=-1, keepdims=True) * scale - slope * float(dil * d)
                     for j, d in new]
            m = jnp.max(s, axis=0)
            for sn in s_new:
                m = jnp.maximum(m, sn)
            p = jnp.exp(s - m[None])
            l = jnp.sum(p, axis=0)
            o = jnp.sum(p * vc, axis=0)
            for (j, _), sn in zip(new, s_new):
                pn = jnp.exp(sn - m)
                l = l + pn
                o = o + pn * qkv_ref[j, 2 * N_DIL + gi]
            outs.append(o / l)
            lses.append(m + jnp.log(l))
        mm = jnp.maximum(jnp.maximum(lses[0], lses[1]), lses[2])
        ws = [jnp.exp(x - mm) for x in lses]
        o_ref[tq] = (ws[0] * outs[0] + ws[1] * outs[1] + ws[2] * outs[2]) / (ws[0] + ws[1] + ws[2])


def _attn_sample(qkv, caches):
    b, t, w = qkv.shape
    qkv5 = qkv.reshape(b, t, 3 * N_DIL, ATT_HPG, ATT_HEAD_DIM)
    specs = [pl.BlockSpec((None, t, 3 * N_DIL, ATT_HPG, ATT_HEAD_DIM), lambda i: (i, 0, 0, 0, 0))]
    views = []
    for (window, dil), cache in zip(DIL_GROUPS, caches):
        lb = cache.shape[1]
        assert lb == window and lb // dil == ATT_BLOCK and (dil == 1 or dil >= t)
        if dil == 1:
            views.append(cache)
            specs.append(pl.BlockSpec((None, lb, 2, ATT_HPG, ATT_HEAD_DIM), lambda i: (i, 0, 0, 0, 0)))
        else:
            views.append(cache.reshape(b, lb // dil, dil, 2, ATT_HPG, ATT_HEAD_DIM))
            specs.append(pl.BlockSpec((None, lb // dil, t, 2, ATT_HPG, ATT_HEAD_DIM), lambda i: (i, 0, 0, 0, 0, 0)))
    kern = functools.partial(_attn_sample_kernel, t=t,
                             slopes=[[float(v) for v in row] for row in _alibi_slopes()])
    att = pl.pallas_call(
        kern,
        grid=(b,),
        in_specs=specs,
        out_specs=pl.BlockSpec((None, t, ATT_HPG, ATT_HEAD_DIM), lambda i: (i, 0, 0, 0)),
        out_shape=jax.ShapeDtypeStruct((b, t, ATT_HPG, ATT_HEAD_DIM), F32),
        compiler_params=_cparams("parallel"),
        name="attn_sample",
    )(qkv5, *views)
    return att.reshape(b, t, ATT_OUT_WIDTH)


def _merge_kernel(att_ref, y_ref, gs_ref, ga_ref, ws_ref, wa_ref, o_ref):
    bs = _dot(y_ref[...], ws_ref[...])
    ba = _dot(att_ref[...].astype(BF16), wa_ref[...])
    o_ref[...] = (_sigmoid(gs_ref[...].astype(F32)) * bs + _sigmoid(ga_ref[...].astype(F32)) * ba).astype(o_ref.dtype)


def _merge(att, y, gates, w_ssm, w_att, bm):
    m = y.shape[0]
    bn = 512
    nj = D_MODEL // bn
    return pl.pallas_call(
        _merge_kernel,
        grid=(m // bm, nj),
        in_specs=[pl.BlockSpec((bm, ATT_OUT_WIDTH), lambda i, j: (i, 0)),
                  pl.BlockSpec((bm, D_INNER), lambda i, j: (i, 0)),
                  pl.BlockSpec((bm, bn), lambda i, j: (i, j)),
                  pl.BlockSpec((bm, bn), lambda i, j: (i, j + nj)),
                  pl.BlockSpec((D_INNER, bn), lambda i, j: (0, j)),
                  pl.BlockSpec((ATT_OUT_WIDTH, bn), lambda i, j: (0, j))],
        out_specs=pl.BlockSpec((bm, bn), lambda i, j: (i, j)),
        out_shape=jax.ShapeDtypeStruct((m, D_MODEL), BF16),
        compiler_params=_cparams("parallel", "arbitrary"),
        name="merge",
    )(att, y, gates, gates, w_ssm, w_att)


def _outproj_kernel(mg_ref, x_ref, gt_ref, gpost_ref, w_ref, o_ref):
    out = _dot(mg_ref[...], w_ref[...])
    o_ref[...] = x_ref[...] + gt_ref[...] * _rms(out, gpost_ref[...])


def _outproj(merged, x, gate, g_post, w_out, bm, rows_per_batch):
    m, d = x.shape
    return pl.pallas_call(
        _outproj_kernel,
        grid=(m // bm,),
        in_specs=[pl.BlockSpec((bm, d), lambda i: (i, 0)),
                  pl.BlockSpec((bm, d), lambda i: (i, 0)),
                  _mod_spec(gate, bm, rows_per_batch), _vec_spec(d),
                  pl.BlockSpec((d, d), lambda i: (0, 0))],
        out_specs=pl.BlockSpec((bm, d), lambda i: (i, 0)),
        out_shape=jax.ShapeDtypeStruct((m, d), F32),
        compiler_params=_cparams("parallel"),
        name="outproj",
    )(merged, x, gate, g_post.reshape(1, d), w_out)


def _layer(x3, mod, conv_prev, ssm_prev, kv_bufs, p, bm):
    b, t, d = x3.shape
    m = b * t
    prompt = kv_bufs is None
    x = x3.reshape(m, d)
    if prompt:
        mods = [mod[:, k].reshape(b, 1, d) for k in range(3 * N_SUB)]
    else:
        mods = [jnp.repeat(mod[:, k], t, axis=0) for k in range(3 * N_SUB)]

    x = _ffn(x, mods[0], mods[1], mods[2], p['g_pre_ffn1'], p['g_post_ffn1'], p['w_gu_ffn1'],
             p['w_down_ffn1'], bm, t)

    h = _prenorm(x, mods[3], mods[4], p['g_pre_mix'], bm, t)
    bmm = min(m, MM_BM)
    z = _mm(h, p['w_z'], BF16, bmm, MM_BN, "inproj_z")
    xbc = _mm(h, p['w_xbc'], BF16, bmm, MM_BN, "inproj_xbc")
    dtr = _mm(h, p['w_dt'], F32, bmm, LANES, "inproj_dt")
    qkv = _mm(h, p['w_qkv'], F32, bmm, ATT_QKV_WIDTH, "inproj_qkv")
    gates = _mm(h, p['w_gates'], BF16, bmm, MM_BN, "inproj_gates")

    xbc3 = xbc.reshape(b, t, CONV_DIM)
    if prompt:
        conv_prev = jnp.zeros((b, CONV_W - 1, CONV_DIM), F32)
        conv_new = xbc3[:, t - (CONV_W - 1):].astype(F32)
    else:
        conv_new = jnp.concatenate([conv_prev, xbc3.astype(F32)], axis=1)[:, -(CONV_W - 1):]
    h0 = None if prompt else ssm_prev.reshape(b, SSM_HEADS * SSM_HEAD_DIM, D_STATE)
    y, ssm_new = _ssd(xbc3, z.reshape(b, t, D_INNER), dtr.reshape(b, t, LANES), conv_prev, h0,
                      p['conv_w'], p['conv_b'], p['dt_bias'], p['a_log'], p['d_skip'], p['g_ssm_norm'])
    ssm_new = ssm_new.reshape(b, SSM_HEADS, SSM_HEAD_DIM, D_STATE)

    qkv3 = qkv.reshape(b, t, 3 * ATT_QKV_WIDTH)
    kv_new = []
    for gi, (window, dil) in enumerate(DIL_GROUPS):
        c0 = ATT_QKV_WIDTH + gi * ATT_OUT_WIDTH
        rows = slice(t - min(window, t), t) if prompt else slice(0, t)
        kg = qkv3[:, rows, c0:c0 + ATT_OUT_WIDTH].reshape(b, -1, ATT_HPG, ATT_HEAD_DIM)
        vg = qkv3[:, rows, ATT_QKV_WIDTH + c0:ATT_QKV_WIDTH + c0 + ATT_OUT_WIDTH].reshape(
            b, -1, ATT_HPG, ATT_HEAD_DIM)
        new = jnp.stack([kg, vg], axis=2)
        if prompt:
            kv_new.append(new)
        else:
            lb = kv_bufs[gi].shape[1]
            kv_new.append(jnp.concatenate([kv_bufs[gi], new], axis=1)[:, -lb:])
    if prompt:
        att = _attn_prompt(qkv3)
    else:
        att = _attn_sample(qkv3, kv_bufs)

    merged = _merge(att.reshape(m, ATT_OUT_WIDTH), y.reshape(m, D_INNER), gates, p['w_ssm_proj'],
                    p['w_att_proj'], bmm)
    x = _outproj(merged, x, mods[5], p['g_post_mix'], p['w_out'], bm, t)

    x = _ffn(x, mods[6], mods[7], mods[8], p['g_pre_ffn2'], p['g_post_ffn2'], p['w_gu_ffn2'],
             p['w_down_ffn2'], bm, t)
    return x.reshape(b, t, d), conv_new, ssm_new, kv_new


def _prep_weights(w_gu_ffn1, w_down_ffn1, w_in, w_ssm_proj, w_att_proj, w_out, w_gu_ffn2, w_down_ffn2):
    o = 0
    segs = {}
    for name, size in (('z', D_INNER), ('xbc', CONV_DIM), ('dt', SSM_HEADS), ('qkv', 3 * ATT_QKV_WIDTH),
                       ('gates', 2 * D_MODEL)):
        segs[name] = w_in[:, o:o + size].astype(BF16)
        o += size
    segs['dt'] = jnp.pad(segs['dt'], ((0, 0), (0, LANES - SSM_HEADS)))
    return {
        'w_gu_ffn1': w_gu_ffn1.astype(BF16), 'w_down_ffn1': w_down_ffn1.astype(BF16),
        'w_gu_ffn2': w_gu_ffn2.astype(BF16), 'w_down_ffn2': w_down_ffn2.astype(BF16),
        'w_z': segs['z'], 'w_xbc': segs['xbc'], 'w_dt': segs['dt'], 'w_qkv': segs['qkv'],
        'w_gates': segs['gates'],
        'w_ssm_proj': w_ssm_proj.astype(BF16), 'w_att_proj': w_att_proj.astype(BF16),
        'w_out': w_out.astype(BF16),
    }


def kernel(x_prompt, x_sample, c_prompt, c_sample, state_ssm, state_conv, cache_kv_w128, cache_kv_w512, cache_kv_w2048, w_ada, b_ada, g_pre_ffn1, g_post_ffn1, w_gu_ffn1, w_down_ffn1, g_pre_mix, g_post_mix, w_in, conv_w, conv_b, dt_bias, a_log, d_skip, g_ssm_norm, w_ssm_proj, w_att_proj, w_out, g_pre_ffn2, g_post_ffn2, w_gu_ffn2, w_down_ffn2):
    depth = w_ada.shape[0]
    kv_in = (cache_kv_w128, cache_kv_w512, cache_kv_w2048)
    bp = x_prompt.shape[0]
    yp, ys = x_prompt, x_sample
    outs = [[] for _ in range(10)]
    for l in range(depth):
        p = _prep_weights(w_gu_ffn1[l], w_down_ffn1[l], w_in[l], w_ssm_proj[l], w_att_proj[l], w_out[l],
                          w_gu_ffn2[l], w_down_ffn2[l])
        p.update({'g_pre_ffn1': g_pre_ffn1[l], 'g_post_ffn1': g_post_ffn1[l], 'g_pre_mix': g_pre_mix[l],
                  'g_post_mix': g_post_mix[l], 'g_pre_ffn2': g_pre_ffn2[l], 'g_post_ffn2': g_post_ffn2[l],
                  'conv_w': conv_w[l], 'conv_b': conv_b[l], 'dt_bias': dt_bias[l], 'a_log': a_log[l],
                  'd_skip': d_skip[l], 'g_ssm_norm': g_ssm_norm[l]})
        c_all = jnp.concatenate([c_prompt, c_sample], axis=0)
        mod = _ada(c_all, w_ada[l], b_ada[l]).reshape(c_all.shape[0], 3 * N_SUB, D_MODEL)
        yp, cp, sp, kp = _layer(yp, mod[:bp], None, None, None, p, 512)
        ys, cs, ss, ksm = _layer(ys, mod[bp:], state_conv[l], state_ssm[l], tuple(kv[l] for kv in kv_in), p,
                                 ys.shape[0] * ys.shape[1])
        for lst, val in zip(outs, (sp, ss, cp, cs, kp[0], ksm[0], kp[1], ksm[1], kp[2], ksm[2])):
            lst.append(val)
    return (yp, ys) + tuple(jnp.stack(o) for o in outs)
```

```python
import functools
import math

import numpy as np
import jax
import jax.numpy as jnp
from jax import lax
from jax.experimental import pallas as pl
from jax.experimental.pallas import tpu as pltpu

D_MODEL = 2048
SSM_HEAD_DIM = 64
D_INNER = 2 * D_MODEL
SSM_HEADS = D_INNER // SSM_HEAD_DIM
SSM_GROUPS = 8
SSM_HPG = SSM_HEADS // SSM_GROUPS
D_STATE = 128
CONV_W = 4
CONV_DIM = D_INNER + 2 * SSM_GROUPS * D_STATE
ATT_HEAD_DIM = 128
DIL_GROUPS = ((128, 1), (512, 4), (2048, 16))
N_DIL = len(DIL_GROUPS)
ATT_HPG = 4
ATT_HEADS = N_DIL * ATT_HPG
ATT_QKV_WIDTH = ATT_HEADS * ATT_HEAD_DIM
ATT_OUT_WIDTH = ATT_HPG * ATT_HEAD_DIM
ATT_BLOCK = 128
ALIBI_MAX_BIAS = 8.0
D_FF = ((8 * D_MODEL // 3 + 255) // 256) * 256
N_SUB = 3
EPS = 1e-6
NEG_INF = -1e30

LANES = 128
SUBLANES = 8
VMEM_LIMIT = 56 * 1024 * 1024
SSD_Q = 128
MM_BM = 1024
MM_BN = 1024

F32 = jnp.float32
BF16 = jnp.bfloat16


def _cparams(*sem):
    return pltpu.CompilerParams(dimension_semantics=sem, vmem_limit_bytes=VMEM_LIMIT)


def _qkv_unit(part, gi):
    return gi if part == 0 else N_DIL + 2 * gi + (part - 1)


def _alibi_slopes():
    i = np.arange(1, ATT_HEADS + 1, dtype=np.float32)
    return np.exp2(-ALIBI_MAX_BIAS * i / ATT_HEADS).astype(np.float32).reshape(N_DIL, ATT_HPG)


def _silu(x):
    return x * (1.0 / (1.0 + jnp.exp(-x)))


def _sigmoid(x):
    return 1.0 / (1.0 + jnp.exp(-x))


def _rms(x, g):
    return x * lax.rsqrt(jnp.mean(x * x, axis=-1, keepdims=True) + EPS) * g


def _split2(x):
    hi = x.astype(BF16)
    lo = (x - hi.astype(F32)).astype(BF16)
    return hi, lo


def _split3(x):
    hi = x.astype(BF16)
    r = x - hi.astype(F32)
    mid = r.astype(BF16)
    lo = (r - mid.astype(F32)).astype(BF16)
    return hi, mid, lo


def _dot(a, b):
    return jnp.dot(a, b, preferred_element_type=F32)


def _dot_nt(a, b):
    return lax.dot_general(a, b, (((1,), (1,)), ((), ())), preferred_element_type=F32)


def _dot_tn(a, b):
    return lax.dot_general(a, b, (((0,), (0,)), ((), ())), preferred_element_type=F32)


def _mod_spec(mod, bm, rows_per_batch):
    if mod.ndim == 3:
        bpb = rows_per_batch // bm
        return pl.BlockSpec((None, 1, D_MODEL), lambda i, *_: (i // bpb, 0, 0))
    return pl.BlockSpec((bm, D_MODEL), lambda i, *_: (i, 0))


def _vec_spec(n):
    return pl.BlockSpec((1, n), lambda *_: (0, 0))


def _ada_kernel(c_ref, w_ref, b_ref, o_ref):
    c = _silu(c_ref[...]).astype(BF16)
    o_ref[...] = _dot(c, w_ref[...].astype(BF16)) + b_ref[...]


def _ada(c, w, b):
    m, k = c.shape
    n = w.shape[1]
    bn = 1024
    return pl.pallas_call(
        _ada_kernel,
        grid=(n // bn,),
        in_specs=[pl.BlockSpec((m, k), lambda j: (0, 0)),
                  pl.BlockSpec((k, bn), lambda j: (0, j)),
                  pl.BlockSpec((1, bn), lambda j: (0, j))],
        out_specs=pl.BlockSpec((m, bn), lambda j: (0, j)),
        out_shape=jax.ShapeDtypeStruct((m, n), F32),
        compiler_params=_cparams("parallel"),
        name="ada",
    )(c, w, b.reshape(1, n))


def _ffn_kernel(x_ref, sh_ref, sc_ref, gt_ref, gpre_ref, gpost_ref, wg_ref, wu_ref, wd_ref, o_ref,
                h_scr, acc_scr):
    j = pl.program_id(1)

    @pl.when(j == 0)
    def _():
        x = x_ref[...]
        h = _rms(x, gpre_ref[...]) * (1.0 + sc_ref[...]) + sh_ref[...]
        h_scr[...] = h.astype(BF16)
        acc_scr[...] = jnp.zeros_like(acc_scr)

    h = h_scr[...]
    g = _dot(h, wg_ref[...])
    u = _dot(h, wu_ref[...])
    a = (_silu(g) * u).astype(BF16)
    acc_scr[...] += _dot(a, wd_ref[...])

    @pl.when(j == pl.num_programs(1) - 1)
    def _():
        o_ref[...] = x_ref[...] + 0.5 * gt_ref[...] * _rms(acc_scr[...], gpost_ref[...])


def _ffn(x, shift, scale, gate, g_pre, g_post, w_gu, w_down, bm, rows_per_batch):
    m, d = x.shape
    ff = w_down.shape[0]
    bf = 512
    nj = ff // bf
    ms = lambda a: _mod_spec(a, bm, rows_per_batch)
    return pl.pallas_call(
        _ffn_kernel,
        grid=(m // bm, nj),
        in_specs=[pl.BlockSpec((bm, d), lambda i, j: (i, 0)),
                  ms(shift), ms(scale), ms(gate), _vec_spec(d), _vec_spec(d),
                  pl.BlockSpec((d, bf), lambda i, j: (0, j)),
                  pl.BlockSpec((d, bf), lambda i, j: (0, j + nj)),
                  pl.BlockSpec((bf, d), lambda i, j: (j, 0))],
        out_specs=pl.BlockSpec((bm, d), lambda i, j: (i, 0)),
        out_shape=jax.ShapeDtypeStruct((m, d), F32),
        scratch_shapes=[pltpu.VMEM((bm, d), BF16), pltpu.VMEM((bm, d), F32)],
        compiler_params=_cparams("parallel", "arbitrary"),
        name="ffn",
    )(x, shift, scale, gate, g_pre.reshape(1, d), g_post.reshape(1, d), w_gu, w_gu, w_down)


def _prenorm_kernel(x_ref, sh_ref, sc_ref, g_ref, o_ref):
    o_ref[...] = (_rms(x_ref[...], g_ref[...]) * (1.0 + sc_ref[...]) + sh_ref[...]).astype(o_ref.dtype)


def _prenorm(x, shift, scale, g, bm, rows_per_batch):
    m, d = x.shape
    ms = lambda a: _mod_spec(a, bm, rows_per_batch)
    return pl.pallas_call(
        _prenorm_kernel,
        grid=(m // bm,),
        in_specs=[pl.BlockSpec((bm, d), lambda i: (i, 0)), ms(shift), ms(scale), _vec_spec(d)],
        out_specs=pl.BlockSpec((bm, d), lambda i: (i, 0)),
        out_shape=jax.ShapeDtypeStruct((m, d), BF16),
        compiler_params=_cparams("parallel"),
        name="prenorm",
    )(x, shift, scale, g.reshape(1, d))


def _mm_kernel(x_ref, w_ref, o_ref):
    o_ref[...] = _dot(x_ref[...], w_ref[...]).astype(o_ref.dtype)


def _mm(x, w, out_dtype, bm, bn, name):
    m, k = x.shape
    n = w.shape[1]
    return pl.pallas_call(
        _mm_kernel,
        grid=(m // bm, n // bn),
        in_specs=[pl.BlockSpec((bm, k), lambda i, j: (i, 0)),
                  pl.BlockSpec((k, bn), lambda i, j: (0, j))],
        out_specs=pl.BlockSpec((bm, bn), lambda i, j: (i, j)),
        out_shape=jax.ShapeDtypeStruct((m, n), out_dtype),
        compiler_params=_cparams("parallel", "arbitrary"),
        name=name,
    )(x, w)


def _ssd_kernel(xbc_ref, xbcn_ref, z_ref, dtr_ref, prev_ref, h0_ref, cw_ref, cb_ref, dtb_ref, alog_ref,
                dskip_ref, gn_ref, e_ref, et_ref, y_ref, hout_ref,
                xp_scr, act_scr, y_scr, h_scr, dt_scr, *, q, nc, t_valid, has_init):
    c = pl.program_id(1)
    w8 = SUBLANES
    qo = xbc_ref.shape[0]
    slab = 512

    def rows8(ref, start):
        if ref.dtype == F32:
            return ref[start:start + w8, :]
        base = start // (2 * w8) * (2 * w8)
        return ref[base:base + 2 * w8, :].astype(F32)[start - base:start - base + w8]

    def conv_chunk(src_ref, act):
        if qo < q:
            assert qo == w8
            act[...] = jnp.zeros((q, CONV_DIM), F32)
        else:
            ri = lax.broadcasted_iota(jnp.int32, (q, q), 0)
            ci = lax.broadcasted_iota(jnp.int32, (q, q), 1)
            shift = jnp.concatenate([jnp.where(ri - ci == CONV_W - 1 - i, 1.0, 0.0) for i in range(CONV_W - 1)],
                                    axis=0).astype(BF16)
            for s in range(CONV_DIM // slab):
                cs = slice(s * slab, (s + 1) * slab)
                xb = src_ref[:, cs]
                sh = _dot(shift, xb)
                acc = cb_ref[:, cs] + cw_ref[CONV_W - 1:CONV_W, cs] * xb.astype(F32)
                for i in range(CONV_W - 1):
                    acc = acc + cw_ref[i:i + 1, cs] * sh[i * q:(i + 1) * q]
                act[:, cs] = _silu(acc)
        xp_scr[w8:2 * w8, :] = rows8(src_ref, 0)
        for s in range(CONV_DIM // slab):
            cs = slice(s * slab, (s + 1) * slab)
            acc = cb_ref[:, cs]
            for i in range(CONV_W):
                off = w8 - (CONV_W - 1) + i
                acc = acc + cw_ref[i:i + 1, cs] * xp_scr[off:off + w8, cs]
            act[0:w8, cs] = _silu(acc)
        xp_scr[0:w8, :] = rows8(src_ref, qo - w8)

    @pl.when(c == 0)
    def _():
        xp_scr[0:w8, :] = jnp.zeros((w8, CONV_DIM), F32)
        xp_scr[w8 - (CONV_W - 1):w8, :] = prev_ref[...]
        if has_init:
            h_scr[...] = h0_ref[...]
        else:
            h_scr[...] = jnp.zeros_like(h_scr)
        conv_chunk(xbc_ref, act_scr.at[0])

    if nc > 1:
        slot = c % 2
        conv_chunk(xbcn_ref, act_scr.at[1 - slot])
        act_cur = act_scr.at[slot]
    else:
        act_cur = act_scr.at[0]

    if qo < q:
        dt_scr[...] = jnp.zeros_like(dt_scr)
        dt_scr[0:qo, :] = dtr_ref[...]
    else:
        dt_scr[...] = dtr_ref[...]

    x = dt_scr[:, 0:SSM_HEADS] + dtb_ref[...]
    dt = jnp.maximum(x, 0.0) + jnp.log1p(jnp.exp(-jnp.abs(x)))
    if t_valid < q:
        rows = lax.broadcasted_iota(jnp.int32, (q, SSM_HEADS), 0)
        dt = jnp.where(rows < t_valid, dt, 0.0)
    a = -jnp.exp(alog_ref[...])
    la = dt * a
    ri = lax.broadcasted_iota(jnp.int32, (q, q), 0)
    ci = lax.broadcasted_iota(jnp.int32, (q, q), 1)
    causal = ri >= ci
    ltri = jnp.where(causal, 1.0, 0.0).astype(BF16)
    utri = jnp.where(ri <= ci, 1.0, 0.0).astype(BF16)
    parts = _split3(la)
    cum = _dot(ltri, parts[0]) + _dot(ltri, parts[1]) + _dot(ltri, parts[2])
    cum_t = _dot_tn(parts[0], utri) + _dot_tn(parts[1], utri) + _dot_tn(parts[2], utri)
    cum_last = cum[q - 1:q, :]
    dend = jnp.exp(cum_last - cum)
    ecum = jnp.exp(cum)

    stack = jnp.concatenate([dt, dt * dend, ecum], axis=0)
    s_hi, s_lo = _split2(stack)
    cdl = jnp.broadcast_to(jnp.exp(cum_t[:, q - 1:q]), (SSM_HEADS, LANES))
    c_hi, c_lo = _split2(cdl)

    lane = lax.broadcasted_iota(jnp.int32, (q, LANES), 1)
    lo_half = lane < SSM_HEAD_DIM
    causal_o = causal[0:qo]
    cum_o = cum[0:qo]
    gw = SSM_HPG * SSM_HEAD_DIM
    for g in range(SSM_GROUPS):
        gs = slice(g * gw, (g + 1) * gw)
        e_g = e_ref[:, gs]
        exp3 = _dot(s_hi, e_g) + _dot(s_lo, e_g)
        xs_g = act_cur[:, gs]
        xdt = (xs_g * exp3[0:q]).astype(BF16)
        xdd = (xs_g * exp3[q:2 * q]).astype(BF16)
        ecum_e = exp3[2 * q:2 * q + qo]
        bg = act_cur[:, D_INNER + g * D_STATE:D_INNER + (g + 1) * D_STATE].astype(BF16)
        cg = act_cur[:, D_INNER + SSM_GROUPS * D_STATE + g * D_STATE:
                     D_INNER + SSM_GROUPS * D_STATE + (g + 1) * D_STATE].astype(BF16)
        cbm = _dot_nt(cg[0:qo], bg)
        hg = h_scr[gs, :]
        yoff = _dot_nt(cg[0:qo], hg.astype(BF16))
        for pr in range(SSM_HPG // 2):
            h0 = g * SSM_HPG + 2 * pr
            a0 = cbm * jnp.exp(jnp.where(causal_o, cum_o[:, h0:h0 + 1] - cum_t[h0:h0 + 1, :], NEG_INF))
            a1 = cbm * jnp.exp(jnp.where(causal_o, cum_o[:, h0 + 1:h0 + 2] - cum_t[h0 + 1:h0 + 2, :], NEG_INF))
            lhs = jnp.concatenate([a0, a1], axis=1).astype(BF16)
            ps = slice(pr * LANES, (pr + 1) * LANES)
            x2 = xdt[:, ps]
            zero = jnp.zeros_like(x2)
            rhs = jnp.concatenate([jnp.where(lo_half, x2, zero), jnp.where(lo_half, zero, x2)], axis=0)
            yd = _dot(lhs, rhs)
            y_scr[:, g * gw + pr * LANES:g * gw + (pr + 1) * LANES] = yd + yoff[:, ps] * ecum_e[:, ps]
        st = _dot_tn(xdd, bg)
        et_g = et_ref[gs, :]
        cdec = _dot(et_g, c_hi) + _dot(et_g, c_lo)
        h_scr[gs, :] = hg * cdec + st

    y = y_scr[...] + act_cur[0:qo, 0:D_INNER] * dskip_ref[...]
    y = _rms(y * _silu(z_ref[...].astype(F32)), gn_ref[...])
    y_ref[...] = y.astype(y_ref.dtype)

    @pl.when(c == pl.num_programs(1) - 1)
    def _():
        hout_ref[...] = h_scr[...]


def _ssd(xbc, z, dtr, conv_prev, h0, conv_w, conv_b, dt_bias, a_log, d_skip, g_norm):
    b, t, _ = xbc.shape
    q = SSD_Q
    t_valid = min(t, q)
    if t >= q:
        assert t % q == 0
        nc, bt = t // q, q
    else:
        bt = -(-t // SUBLANES) * SUBLANES
        padt = lambda v: jnp.pad(v.astype(F32), ((0, 0), (0, bt - t), (0, 0)))
        xbc, z, dtr = padt(xbc), padt(z), padt(dtr)
        nc = 1
    has_init = h0 is not None
    if not has_init:
        h0 = jnp.zeros((b, SUBLANES, D_STATE), F32)
    hp = SSM_HEADS * SSM_HEAD_DIM
    e = jnp.asarray(np.kron(np.eye(SSM_HEADS, dtype=np.float32), np.ones((1, SSM_HEAD_DIM), np.float32)), BF16)
    et = jnp.asarray(np.kron(np.eye(SSM_HEADS, dtype=np.float32), np.ones((SSM_HEAD_DIM, 1), np.float32)), BF16)
    dskip_e = jnp.repeat(d_skip, SSM_HEAD_DIM).reshape(1, D_INNER)
    kern = functools.partial(_ssd_kernel, q=q, nc=nc, t_valid=t_valid, has_init=has_init)
    h0_block = (None, hp, D_STATE) if has_init else (None, SUBLANES, D_STATE)
    y, h = pl.pallas_call(
        kern,
        grid=(b, nc),
        in_specs=[pl.BlockSpec((None, bt, CONV_DIM), lambda i, c: (i, c, 0)),
                  pl.BlockSpec((None, bt, CONV_DIM), lambda i, c: (i, jnp.minimum(c + 1, nc - 1), 0)),
                  pl.BlockSpec((None, bt, D_INNER), lambda i, c: (i, c, 0)),
                  pl.BlockSpec((None, bt, LANES), lambda i, c: (i, c, 0)),
                  pl.BlockSpec((None, CONV_W - 1, CONV_DIM), lambda i, c: (i, 0, 0)),
                  pl.BlockSpec(h0_block, lambda i, c: (i, 0, 0)),
                  pl.BlockSpec((CONV_W, CONV_DIM), lambda i, c: (0, 0)),
                  _vec_spec(CONV_DIM), _vec_spec(SSM_HEADS), _vec_spec(SSM_HEADS),
                  _vec_spec(D_INNER), _vec_spec(D_INNER),
                  pl.BlockSpec((SSM_HEADS, D_INNER), lambda i, c: (0, 0)),
                  pl.BlockSpec((D_INNER, SSM_HEADS), lambda i, c: (0, 0))],
        out_specs=[pl.BlockSpec((None, bt, D_INNER), lambda i, c: (i, c, 0)),
                   pl.BlockSpec((None, hp, D_STATE), lambda i, c: (i, 0, 0))],
        out_shape=[jax.ShapeDtypeStruct((b, bt * nc, D_INNER), BF16),
                   jax.ShapeDtypeStruct((b, hp, D_STATE), F32)],
        scratch_shapes=[pltpu.VMEM((2 * SUBLANES, CONV_DIM), F32),
                        pltpu.VMEM((min(nc, 2), q, CONV_DIM), F32),
                        pltpu.VMEM((bt, D_INNER), F32),
                        pltpu.VMEM((hp, D_STATE), F32),
                        pltpu.VMEM((q, LANES), F32)],
        compiler_params=_cparams("parallel", "arbitrary"),
        name="ssd",
    )(xbc, xbc, z, dtr, conv_prev, h0, conv_w, conv_b.reshape(1, CONV_DIM), dt_bias.reshape(1, SSM_HEADS),
      a_log.reshape(1, SSM_HEADS), dskip_e, g_norm.reshape(1, D_INNER), e, et)
    return y[:, :t], h


def _attn_prompt_kernel(*refs, slopes, seq):
    qkv_refs = refs[:3 * N_DIL]
    o_ref, og_scr, lg_scr = refs[3 * N_DIL:]
    blk = ATT_BLOCK
    h = pl.program_id(1)
    qi = lax.broadcasted_iota(jnp.int32, (blk, 2 * blk), 0)
    ki = lax.broadcasted_iota(jnp.int32, (blk, 2 * blk), 1)
    dist = qi + blk - ki
    scale = ATT_HEAD_DIM ** -0.5
    for gi, (window, dil) in enumerate(DIL_GROUPS):
        q_ref, k_ref, v_ref = qkv_refs[gi], qkv_refs[N_DIL + gi], qkv_refs[2 * N_DIL + gi]
        slope = slopes[gi][0]
        for hh in range(1, ATT_HPG):
            slope = jnp.where(h == hh, slopes[gi][hh], slope)
        valid2 = (dist >= 0) & (dist <= window // dil)
        bias_any = jnp.where(valid2, -slope * (dil * dist).astype(F32), NEG_INF)
        bias_first = jnp.where(ki >= blk, bias_any, NEG_INF)
        nb = seq // dil // blk

        def rows(start, size):
            return pl.ds(start, size) if dil == 1 else pl.ds(start, size, stride=dil)

        starts = [r + dil * n * blk for r in range(dil) for n in range(nb)]
        firsts = [n == 0 for r in range(dil) for n in range(nb)]

        def band(ref, q0, first):
            if first:
                cur = ref[rows(q0, blk), :].astype(BF16)
                return jnp.concatenate([cur, cur], axis=0)
            return ref[rows(q0 - dil * blk, 2 * blk), :].astype(BF16)

        qs = jnp.stack([q_ref[rows(q0, blk), :].astype(BF16) for q0 in starts])
        ks = jnp.stack([band(k_ref, q0, f) for q0, f in zip(starts, firsts)])
        vs = jnp.stack([band(v_ref, q0, f) for q0, f in zip(starts, firsts)])
        bias = jnp.stack([bias_first if f else bias_any for f in firsts])
        s = jnp.einsum('bqd,bkd->bqk', qs, ks, preferred_element_type=F32) * scale + bias
        m = jnp.max(s, axis=-1, keepdims=True)
        p = jnp.exp(s - m)
        l = jnp.sum(p, axis=-1, keepdims=True)
        o = jnp.einsum('bqk,bkd->bqd', p.astype(BF16), vs, preferred_element_type=F32) / l
        lse = m + jnp.log(l)
        for idx, q0 in enumerate(starts):
            og_scr[gi, rows(q0, blk), :] = o[idx]
            lg_scr[gi, rows(q0, blk), :] = jnp.broadcast_to(lse[idx], (blk, ATT_HEAD_DIM))

    l0, l1, l2 = lg_scr[0], lg_scr[1], lg_scr[2]
    mm = jnp.maximum(jnp.maximum(l0, l1), l2)
    w0, w1, w2 = jnp.exp(l0 - mm), jnp.exp(l1 - mm), jnp.exp(l2 - mm)
    att = (w0 * og_scr[0] + w1 * og_scr[1] + w2 * og_scr[2]) / (w0 + w1 + w2)
    o_ref[...] = att.astype(o_ref.dtype)


def _attn_prompt(qkv):
    b, s, w = qkv.shape
    for window, dil in DIL_GROUPS:
        assert s % (dil * ATT_BLOCK) == 0
    hd = ATT_HEAD_DIM
    specs = [pl.BlockSpec((None, s, hd), functools.partial(lambda i, h, c: (i, 0, c + h), c=_qkv_unit(part, gi) * ATT_HPG))
             for part in range(3) for gi in range(N_DIL)]
    kern = functools.partial(_attn_prompt_kernel, seq=s,
                             slopes=[[float(v) for v in row] for row in _alibi_slopes()])
    return pl.pallas_call(
        kern,
        grid=(b, ATT_HPG),
        in_specs=specs,
        out_specs=pl.BlockSpec((None, s, hd), lambda i, h: (i, 0, h)),
        out_shape=jax.ShapeDtypeStruct((b, s, ATT_OUT_WIDTH), BF16),
        scratch_shapes=[pltpu.VMEM((N_DIL, s, hd), F32), pltpu.VMEM((N_DIL, s, hd), F32)],
        compiler_params=_cparams("parallel", "arbitrary"),
        name="attn_prompt",
    )(*([qkv] * (3 * N_DIL)))


def _attn_sample_kernel(qkv_ref, c0_ref, c1_ref, c2_ref, o_ref, *, t, slopes):
    scale = ATT_HEAD_DIM ** -0.5
    caches = (c0_ref, c1_ref, c2_ref)
    nk = ATT_BLOCK
    step = lax.broadcasted_iota(jnp.int32, (nk, ATT_HPG, 1), 0)
    hid = lax.broadcasted_iota(jnp.int32, (ATT_HPG, 1), 0)
    for tq in range(t):
        outs, lses = [], []
        for gi, (window, dil) in enumerate(DIL_GROUPS):
            kmax = window // dil
            slope = jnp.full((ATT_HPG, 1), slopes[gi][0], F32)
            for hh in range(1, ATT_HPG):
                slope = jnp.where(hid == hh, slopes[gi][hh], slope)
            q = qkv_ref[tq, _qkv_unit(0, gi)]
            cref = caches[gi]
            if dil == 1:
                kc, vc = cref[:, 0], cref[:, 1]
                dist = kmax + tq - step
                new = [(j, tq - j) for j in range(tq + 1)]
            else:
                kc, vc = cref[:, tq, 0], cref[:, tq, 1]
                dist = kmax - step
                new = [(tq, 0)]
            s = jnp.sum(kc * q[None], axis=-1, keepdims=True) * scale - slope[None] * (dil * dist).astype(F32)
            s = jnp.where(dist <= kmax, s, NEG_INF)
            s_new = [jnp.sum(qkv_ref[j, _qkv_unit(1, gi)] * q, axis=-1, keepdims=True) * scale - slope * float(dil * d)
                     for j, d in new]
            m = jnp.max(s, axis=0)
            for sn in s_new:
                m = jnp.maximum(m, sn)
            p = jnp.exp(s - m[None])
            l = jnp.sum(p, axis=0)
            o = jnp.sum(p * vc, axis=0)
            for (j, _), sn in zip(new, s_new):
                pn = jnp.exp(sn - m)
                l = l + pn
                o = o + pn * qkv_ref[j, _qkv_unit(2, gi)]
            outs.append(o / l)
            lses.append(m + jnp.log(l))
        mm = jnp.maximum(jnp.maximum(lses[0], lses[1]), lses[2])
        ws = [jnp.exp(x - mm) for x in lses]
        o_ref[tq] = (ws[0] * outs[0] + ws[1] * outs[1] + ws[2] * outs[2]) / (ws[0] + ws[1] + ws[2])


def _attn_sample(qkv, caches):
    b, t, w = qkv.shape
    qkv5 = qkv.reshape(b, t, 3 * N_DIL, ATT_HPG, ATT_HEAD_DIM)
    specs = [pl.BlockSpec((None, t, 3 * N_DIL, ATT_HPG, ATT_HEAD_DIM), lambda i: (i, 0, 0, 0, 0))]
    views = []
    for (window, dil), cache in zip(DIL_GROUPS, caches):
        lb = cache.shape[1]
        assert lb == window and lb // dil == ATT_BLOCK and (dil == 1 or dil >= t)
        if dil == 1:
            views.append(cache)
            specs.append(pl.BlockSpec((None, lb, 2, ATT_HPG, ATT_HEAD_DIM), lambda i: (i, 0, 0, 0, 0)))
        else:
            views.append(cache.reshape(b, lb // dil, dil, 2, ATT_HPG, ATT_HEAD_DIM))
            specs.append(pl.BlockSpec((None, lb // dil, t, 2, ATT_HPG, ATT_HEAD_DIM), lambda i: (i, 0, 0, 0, 0, 0)))
    kern = functools.partial(_attn_sample_kernel, t=t,
                             slopes=[[float(v) for v in row] for row in _alibi_slopes()])
    att = pl.pallas_call(
        kern,
        grid=(b,),
        in_specs=specs,
        out_specs=pl.BlockSpec((None, t, ATT_HPG, ATT_HEAD_DIM), lambda i: (i, 0, 0, 0)),
        out_shape=jax.ShapeDtypeStruct((b, t, ATT_HPG, ATT_HEAD_DIM), F32),
        compiler_params=_cparams("parallel"),
        name="attn_sample",
    )(qkv5, *views)
    return att.reshape(b, t, ATT_OUT_WIDTH)


def _merge_kernel(att_ref, y_ref, gs_ref, ga_ref, ws_ref, wa_ref, o_ref):
    bs = _dot(y_ref[...], ws_ref[...])
    ba = _dot(att_ref[...].astype(BF16), wa_ref[...])
    o_ref[...] = (_sigmoid(gs_ref[...].astype(F32)) * bs + _sigmoid(ga_ref[...].astype(F32)) * ba).astype(o_ref.dtype)


def _merge(att, y, gates, w_ssm, w_att, bm):
    m = y.shape[0]
    bn = 512
    nj = D_MODEL // bn
    return pl.pallas_call(
        _merge_kernel,
        grid=(m // bm, nj),
        in_specs=[pl.BlockSpec((bm, ATT_OUT_WIDTH), lambda i, j: (i, 0)),
                  pl.BlockSpec((bm, D_INNER), lambda i, j: (i, 0)),
                  pl.BlockSpec((bm, bn), lambda i, j: (i, j)),
                  pl.BlockSpec((bm, bn), lambda i, j: (i, j + nj)),
                  pl.BlockSpec((D_INNER, bn), lambda i, j: (0, j)),
                  pl.BlockSpec((ATT_OUT_WIDTH, bn), lambda i, j: (0, j))],
        out_specs=pl.BlockSpec((bm, bn), lambda i, j: (i, j)),
        out_shape=jax.ShapeDtypeStruct((m, D_MODEL), BF16),
        compiler_params=_cparams("parallel", "arbitrary"),
        name="merge",
    )(att, y, gates, gates, w_ssm, w_att)


def _outproj_kernel(mg_ref, x_ref, gt_ref, gpost_ref, w_ref, o_ref):
    out = _dot(mg_ref[...], w_ref[...])
    o_ref[...] = x_ref[...] + gt_ref[...] * _rms(out, gpost_ref[...])


def _outproj(merged, x, gate, g_post, w_out, bm, rows_per_batch):
    m, d = x.shape
    return pl.pallas_call(
        _outproj_kernel,
        grid=(m // bm,),
        in_specs=[pl.BlockSpec((bm, d), lambda i: (i, 0)),
                  pl.BlockSpec((bm, d), lambda i: (i, 0)),
                  _mod_spec(gate, bm, rows_per_batch), _vec_spec(d),
                  pl.BlockSpec((d, d), lambda i: (0, 0))],
        out_specs=pl.BlockSpec((bm, d), lambda i: (i, 0)),
        out_shape=jax.ShapeDtypeStruct((m, d), F32),
        compiler_params=_cparams("parallel"),
        name="outproj",
    )(merged, x, gate, g_post.reshape(1, d), w_out)


def _layer(x3, mod, conv_prev, ssm_prev, kv_bufs, p, bm):
    b, t, d = x3.shape
    m = b * t
    prompt = kv_bufs is None
    x = x3.reshape(m, d)
    if prompt:
        mods = [mod[:, k].reshape(b, 1, d) for k in range(3 * N_SUB)]
    else:
        mods = [jnp.repeat(mod[:, k], t, axis=0) for k in range(3 * N_SUB)]

    x = _ffn(x, mods[0], mods[1], mods[2], p['g_pre_ffn1'], p['g_post_ffn1'], p['w_gu_ffn1'],
             p['w_down_ffn1'], bm, t)

    h = _prenorm(x, mods[3], mods[4], p['g_pre_mix'], bm, t)
    bmm = min(m, MM_BM)
    z = _mm(h, p['w_z'], BF16, bmm, MM_BN, "inproj_z")
    xbc = _mm(h, p['w_xbc'], BF16, bmm, MM_BN, "inproj_xbc")
    dtr = _mm(h, p['w_dt'], F32, bmm, LANES, "inproj_dt")
    qkv = _mm(h, p['w_qkv'], F32, bmm, ATT_QKV_WIDTH, "inproj_qkv")
    gates = _mm(h, p['w_gates'], BF16, bmm, MM_BN, "inproj_gates")

    xbc3 = xbc.reshape(b, t, CONV_DIM)
    if prompt:
        conv_prev = jnp.zeros((b, CONV_W - 1, CONV_DIM), F32)
        conv_new = xbc3[:, t - (CONV_W - 1):].astype(F32)
    else:
        conv_new = jnp.concatenate([conv_prev, xbc3.astype(F32)], axis=1)[:, -(CONV_W - 1):]
    h0 = None if prompt else ssm_prev.reshape(b, SSM_HEADS * SSM_HEAD_DIM, D_STATE)
    y, ssm_new = _ssd(xbc3, z.reshape(b, t, D_INNER), dtr.reshape(b, t, LANES), conv_prev, h0,
                      p['conv_w'], p['conv_b'], p['dt_bias'], p['a_log'], p['d_skip'], p['g_ssm_norm'])
    ssm_new = ssm_new.reshape(b, SSM_HEADS, SSM_HEAD_DIM, D_STATE)

    qkv3 = qkv.reshape(b, t, 3 * ATT_QKV_WIDTH)
    kv_new = []
    for gi, (window, dil) in enumerate(DIL_GROUPS):
        c0 = _qkv_unit(1, gi) * ATT_OUT_WIDTH
        rows = slice(t - min(window, t), t) if prompt else slice(0, t)
        new = qkv3[:, rows, c0:c0 + 2 * ATT_OUT_WIDTH].reshape(b, -1, 2, ATT_HPG, ATT_HEAD_DIM)
        if prompt:
            kv_new.append(new)
        else:
            lb = kv_bufs[gi].shape[1]
            kv_new.append(jnp.concatenate([kv_bufs[gi], new], axis=1)[:, -lb:])
    if prompt:
        att = _attn_prompt(qkv3)
    else:
        att = _attn_sample(qkv3, kv_bufs)

    merged = _merge(att.reshape(m, ATT_OUT_WIDTH), y.reshape(m, D_INNER), gates, p['w_ssm_proj'],
                    p['w_att_proj'], bmm)
    x = _outproj(merged, x, mods[5], p['g_post_mix'], p['w_out'], bm, t)

    x = _ffn(x, mods[6], mods[7], mods[8], p['g_pre_ffn2'], p['g_post_ffn2'], p['w_gu_ffn2'],
             p['w_down_ffn2'], bm, t)
    return x.reshape(b, t, d), conv_new, ssm_new, kv_new


def _cast_cols_kernel(*refs, shift):
    o_ref = refs[-1]
    if shift:
        w = jnp.concatenate([refs[0][:, shift:], refs[1][:, :shift]], axis=1)
    else:
        w = refs[0][...]
    o_ref[...] = w.astype(o_ref.dtype)


def _cast_cols(w, start, ncols, bn, out_order=None):
    k = w.shape[0]
    shift = start % LANES
    base = start - shift
    assert base % bn == 0 and ncols % bn == 0 and shift in (0, LANES // 2)
    off = base // bn
    specs = [pl.BlockSpec((k, bn), lambda j: (0, off + j))]
    if shift:
        specs.append(pl.BlockSpec((k, bn), lambda j: (0, off + j + 1)))

    def out_block(j):
        if out_order is None:
            return (0, j)
        dst = out_order[0]
        for src in range(1, len(out_order)):
            dst = jnp.where(j == src, out_order[src], dst)
        return (0, dst)

    return pl.pallas_call(
        functools.partial(_cast_cols_kernel, shift=shift),
        grid=(ncols // bn,),
        in_specs=specs,
        out_specs=pl.BlockSpec((k, bn), out_block),
        out_shape=jax.ShapeDtypeStruct((k, ncols), BF16),
        compiler_params=_cparams("parallel"),
        name="cast_cols",
    )(*([w] * len(specs)))


def _prep_weights(w_gu_ffn1, w_down_ffn1, w_in, w_ssm_proj, w_att_proj, w_out, w_gu_ffn2, w_down_ffn2):
    o = 0
    segs = {}
    qkv_order = [_qkv_unit(part, gi) for part in range(3) for gi in range(N_DIL)]
    for name, size, bn, order in (('z', D_INNER, 1024, None), ('xbc', CONV_DIM, 1024, None),
                                  ('dt', SSM_HEADS, LANES, None),
                                  ('qkv', 3 * ATT_QKV_WIDTH, ATT_OUT_WIDTH, qkv_order),
                                  ('gates', 2 * D_MODEL, 512, None)):
        segs[name] = _cast_cols(w_in, o, max(size, LANES), bn, order)
        o += size
    return {
        'w_gu_ffn1': w_gu_ffn1.astype(BF16), 'w_down_ffn1': w_down_ffn1.astype(BF16),
        'w_gu_ffn2': w_gu_ffn2.astype(BF16), 'w_down_ffn2': w_down_ffn2.astype(BF16),
        'w_z': segs['z'], 'w_xbc': segs['xbc'], 'w_dt': segs['dt'], 'w_qkv': segs['qkv'],
        'w_gates': segs['gates'],
        'w_ssm_proj': w_ssm_proj.astype(BF16), 'w_att_proj': w_att_proj.astype(BF16),
        'w_out': w_out.astype(BF16),
    }


def kernel(x_prompt, x_sample, c_prompt, c_sample, state_ssm, state_conv, cache_kv_w128, cache_kv_w512, cache_kv_w2048, w_ada, b_ada, g_pre_ffn1, g_post_ffn1, w_gu_ffn1, w_down_ffn1, g_pre_mix, g_post_mix, w_in, conv_w, conv_b, dt_bias, a_log, d_skip, g_ssm_norm, w_ssm_proj, w_att_proj, w_out, g_pre_ffn2, g_post_ffn2, w_gu_ffn2, w_down_ffn2):
    depth = w_ada.shape[0]
    kv_in = (cache_kv_w128, cache_kv_w512, cache_kv_w2048)
    bp = x_prompt.shape[0]
    yp, ys = x_prompt, x_sample
    outs = [[] for _ in range(10)]
    for l in range(depth):
        p = _prep_weights(w_gu_ffn1[l], w_down_ffn1[l], w_in[l], w_ssm_proj[l], w_att_proj[l], w_out[l],
                          w_gu_ffn2[l], w_down_ffn2[l])
        p.update({'g_pre_ffn1': g_pre_ffn1[l], 'g_post_ffn1': g_post_ffn1[l], 'g_pre_mix': g_pre_mix[l],
                  'g_post_mix': g_post_mix[l], 'g_pre_ffn2': g_pre_ffn2[l], 'g_post_ffn2': g_post_ffn2[l],
                  'conv_w': conv_w[l], 'conv_b': conv_b[l], 'dt_bias': dt_bias[l], 'a_log': a_log[l],
                  'd_skip': d_skip[l], 'g_ssm_norm': g_ssm_norm[l]})
        c_all = jnp.concatenate([c_prompt, c_sample], axis=0)
        mod = _ada(c_all, w_ada[l], b_ada[l]).reshape(c_all.shape[0], 3 * N_SUB, D_MODEL)
        yp, cp, sp, kp = _layer(yp, mod[:bp], None, None, None, p, 512)
        ys, cs, ss, ksm = _layer(ys, mod[bp:], state_conv[l], state_ssm[l], tuple(kv[l] for kv in kv_in), p,
                                 ys.shape[0] * ys.shape[1])
        for lst, val in zip(outs, (sp, ss, cp, cs, kp[0], ksm[0], kp[1], ksm[1], kp[2], ksm[2])):
            lst.append(val)
    return (yp, ys) + tuple(jnp.stack(o) for o in outs)
```

```python
import functools
import math

import numpy as np
import jax
import jax.numpy as jnp
from jax import lax
from jax.experimental import pallas as pl
from jax.experimental.pallas import tpu as pltpu

D_MODEL = 2048
SSM_HEAD_DIM = 64
D_INNER = 2 * D_MODEL
SSM_HEADS = D_INNER // SSM_HEAD_DIM
SSM_GROUPS = 8
SSM_HPG = SSM_HEADS // SSM_GROUPS
D_STATE = 128
CONV_W = 4
CONV_DIM = D_INNER + 2 * SSM_GROUPS * D_STATE
ATT_HEAD_DIM = 128
DIL_GROUPS = ((128, 1), (512, 4), (2048, 16))
N_DIL = len(DIL_GROUPS)
ATT_HPG = 4
ATT_HEADS = N_DIL * ATT_HPG
ATT_QKV_WIDTH = ATT_HEADS * ATT_HEAD_DIM
ATT_OUT_WIDTH = ATT_HPG * ATT_HEAD_DIM
ATT_BLOCK = 128
ALIBI_MAX_BIAS = 8.0
D_FF = ((8 * D_MODEL // 3 + 255) // 256) * 256
N_SUB = 3
EPS = 1e-6
NEG_INF = -1e30

LANES = 128
SUBLANES = 8
VMEM_LIMIT = 56 * 1024 * 1024
SSD_Q = 128
MM_BM = 1024
MM_BN = 1024

F32 = jnp.float32
BF16 = jnp.bfloat16


def _cparams(*sem):
    return pltpu.CompilerParams(dimension_semantics=sem, vmem_limit_bytes=VMEM_LIMIT)


def _qkv_unit(part, gi):
    return gi if part == 0 else N_DIL + 2 * gi + (part - 1)


def _alibi_slopes():
    i = np.arange(1, ATT_HEADS + 1, dtype=np.float32)
    return np.exp2(-ALIBI_MAX_BIAS * i / ATT_HEADS).astype(np.float32).reshape(N_DIL, ATT_HPG)


def _silu(x):
    return x * (1.0 / (1.0 + jnp.exp(-x)))


def _sigmoid(x):
    return 1.0 / (1.0 + jnp.exp(-x))


def _rms(x, g):
    return x * lax.rsqrt(jnp.mean(x * x, axis=-1, keepdims=True) + EPS) * g


def _split2(x):
    hi = x.astype(BF16)
    lo = (x - hi.astype(F32)).astype(BF16)
    return hi, lo


def _split3(x):
    hi = x.astype(BF16)
    r = x - hi.astype(F32)
    mid = r.astype(BF16)
    lo = (r - mid.astype(F32)).astype(BF16)
    return hi, mid, lo


def _dot(a, b):
    return jnp.dot(a, b, preferred_element_type=F32)


def _dot_nt(a, b):
    return lax.dot_general(a, b, (((1,), (1,)), ((), ())), preferred_element_type=F32)


def _dot_tn(a, b):
    return lax.dot_general(a, b, (((0,), (0,)), ((), ())), preferred_element_type=F32)


def _mod_spec(mod, bm, rows_per_batch):
    if mod.ndim == 3:
        bpb = rows_per_batch // bm
        return pl.BlockSpec((None, 1, D_MODEL), lambda i, *_: (i // bpb, 0, 0))
    return pl.BlockSpec((bm, D_MODEL), lambda i, *_: (i, 0))


def _vec_spec(n):
    return pl.BlockSpec((1, n), lambda *_: (0, 0))


def _ada_kernel(c_ref, w_ref, b_ref, o_ref):
    c = _silu(c_ref[...]).astype(BF16)
    o_ref[...] = _dot(c, w_ref[...].astype(BF16)) + b_ref[...]


def _ada(c, w, b):
    m, k = c.shape
    n = w.shape[1]
    bn = 1024
    return pl.pallas_call(
        _ada_kernel,
        grid=(n // bn,),
        in_specs=[pl.BlockSpec((m, k), lambda j: (0, 0)),
                  pl.BlockSpec((k, bn), lambda j: (0, j)),
                  pl.BlockSpec((1, bn), lambda j: (0, j))],
        out_specs=pl.BlockSpec((m, bn), lambda j: (0, j)),
        out_shape=jax.ShapeDtypeStruct((m, n), F32),
        compiler_params=_cparams("parallel"),
        name="ada",
    )(c, w, b.reshape(1, n))


def _ffn_kernel(x_ref, sh_ref, sc_ref, gt_ref, gpre_ref, gpost_ref, wg_ref, wu_ref, wd_ref, o_ref,
                h_scr, acc_scr):
    j = pl.program_id(1)

    @pl.when(j == 0)
    def _():
        x = x_ref[...]
        h = _rms(x, gpre_ref[...]) * (1.0 + sc_ref[...]) + sh_ref[...]
        h_scr[...] = h.astype(BF16)
        acc_scr[...] = jnp.zeros_like(acc_scr)

    h = h_scr[...]
    g = _dot(h, wg_ref[...])
    u = _dot(h, wu_ref[...])
    a = (_silu(g) * u).astype(BF16)
    acc_scr[...] += _dot(a, wd_ref[...])

    @pl.when(j == pl.num_programs(1) - 1)
    def _():
        o_ref[...] = x_ref[...] + 0.5 * gt_ref[...] * _rms(acc_scr[...], gpost_ref[...])


def _ffn(x, shift, scale, gate, g_pre, g_post, w_gu, w_down, bm, rows_per_batch):
    m, d = x.shape
    ff = w_down.shape[0]
    bf = 512
    nj = ff // bf
    ms = lambda a: _mod_spec(a, bm, rows_per_batch)
    return pl.pallas_call(
        _ffn_kernel,
        grid=(m // bm, nj),
        in_specs=[pl.BlockSpec((bm, d), lambda i, j: (i, 0)),
                  ms(shift), ms(scale), ms(gate), _vec_spec(d), _vec_spec(d),
                  pl.BlockSpec((d, bf), lambda i, j: (0, j)),
                  pl.BlockSpec((d, bf), lambda i, j: (0, j + nj)),
                  pl.BlockSpec((bf, d), lambda i, j: (j, 0))],
        out_specs=pl.BlockSpec((bm, d), lambda i, j: (i, 0)),
        out_shape=jax.ShapeDtypeStruct((m, d), F32),
        scratch_shapes=[pltpu.VMEM((bm, d), BF16), pltpu.VMEM((bm, d), F32)],
        compiler_params=_cparams("parallel", "arbitrary"),
        name="ffn",
    )(x, shift, scale, gate, g_pre.reshape(1, d), g_post.reshape(1, d), w_gu, w_gu, w_down)


def _prenorm_kernel(x_ref, sh_ref, sc_ref, g_ref, o_ref):
    o_ref[...] = (_rms(x_ref[...], g_ref[...]) * (1.0 + sc_ref[...]) + sh_ref[...]).astype(o_ref.dtype)


def _prenorm(x, shift, scale, g, bm, rows_per_batch):
    m, d = x.shape
    ms = lambda a: _mod_spec(a, bm, rows_per_batch)
    return pl.pallas_call(
        _prenorm_kernel,
        grid=(m // bm,),
        in_specs=[pl.BlockSpec((bm, d), lambda i: (i, 0)), ms(shift), ms(scale), _vec_spec(d)],
        out_specs=pl.BlockSpec((bm, d), lambda i: (i, 0)),
        out_shape=jax.ShapeDtypeStruct((m, d), BF16),
        compiler_params=_cparams("parallel"),
        name="prenorm",
    )(x, shift, scale, g.reshape(1, d))


def _mm_kernel(x_ref, w_ref, o_ref):
    o_ref[...] = _dot(x_ref[...], w_ref[...]).astype(o_ref.dtype)


def _mm(x, w, out_dtype, bm, bn, name):
    m, k = x.shape
    n = w.shape[1]
    return pl.pallas_call(
        _mm_kernel,
        grid=(m // bm, n // bn),
        in_specs=[pl.BlockSpec((bm, k), lambda i, j: (i, 0)),
                  pl.BlockSpec((k, bn), lambda i, j: (0, j))],
        out_specs=pl.BlockSpec((bm, bn), lambda i, j: (i, j)),
        out_shape=jax.ShapeDtypeStruct((m, n), out_dtype),
        compiler_params=_cparams("parallel", "arbitrary"),
        name=name,
    )(x, w)


def _ssd_kernel(xbc_ref, xbcn_ref, z_ref, dtr_ref, prev_ref, h0_ref, cw_ref, cb_ref, dtb_ref, alog_ref,
                dskip_ref, gn_ref, e_ref, et_ref, y_ref, hout_ref,
                xp_scr, act_scr, y_scr, h_scr, dt_scr, *, q, nc, t_valid, has_init):
    c = pl.program_id(1)
    w8 = SUBLANES
    qo = xbc_ref.shape[0]
    slab = 512

    def rows8(ref, start):
        if ref.dtype == F32:
            return ref[start:start + w8, :]
        base = start // (2 * w8) * (2 * w8)
        return ref[base:base + 2 * w8, :].astype(F32)[start - base:start - base + w8]

    def conv_chunk(src_ref, act):
        if qo < q:
            assert qo == w8
            act[...] = jnp.zeros((q, CONV_DIM), F32)
        else:
            ri = lax.broadcasted_iota(jnp.int32, (q, q), 0)
            ci = lax.broadcasted_iota(jnp.int32, (q, q), 1)
            shift = jnp.concatenate([jnp.where(ri - ci == CONV_W - 1 - i, 1.0, 0.0) for i in range(CONV_W - 1)],
                                    axis=0).astype(BF16)
            for s in range(CONV_DIM // slab):
                cs = slice(s * slab, (s + 1) * slab)
                xb = src_ref[:, cs]
                sh = _dot(shift, xb)
                acc = cb_ref[:, cs] + cw_ref[CONV_W - 1:CONV_W, cs] * xb.astype(F32)
                for i in range(CONV_W - 1):
                    acc = acc + cw_ref[i:i + 1, cs] * sh[i * q:(i + 1) * q]
                act[:, cs] = _silu(acc)
        xp_scr[w8:2 * w8, :] = rows8(src_ref, 0)
        for s in range(CONV_DIM // slab):
            cs = slice(s * slab, (s + 1) * slab)
            acc = cb_ref[:, cs]
            for i in range(CONV_W):
                off = w8 - (CONV_W - 1) + i
                acc = acc + cw_ref[i:i + 1, cs] * xp_scr[off:off + w8, cs]
            act[0:w8, cs] = _silu(acc)
        xp_scr[0:w8, :] = rows8(src_ref, qo - w8)

    @pl.when(c == 0)
    def _():
        xp_scr[0:w8, :] = jnp.zeros((w8, CONV_DIM), F32)
        xp_scr[w8 - (CONV_W - 1):w8, :] = prev_ref[...]
        if has_init:
            h_scr[...] = h0_ref[...]
        else:
            h_scr[...] = jnp.zeros_like(h_scr)
        conv_chunk(xbc_ref, act_scr.at[0])

    if nc > 1:
        slot = c % 2
        conv_chunk(xbcn_ref, act_scr.at[1 - slot])
        act_cur = act_scr.at[slot]
    else:
        act_cur = act_scr.at[0]

    if qo < q:
        dt_scr[...] = jnp.zeros_like(dt_scr)
        dt_scr[0:qo, :] = dtr_ref[...]
    else:
        dt_scr[...] = dtr_ref[...]

    x = dt_scr[:, 0:SSM_HEADS] + dtb_ref[...]
    dt = jnp.maximum(x, 0.0) + jnp.log1p(jnp.exp(-jnp.abs(x)))
    if t_valid < q:
        rows = lax.broadcasted_iota(jnp.int32, (q, SSM_HEADS), 0)
        dt = jnp.where(rows < t_valid, dt, 0.0)
    a = -jnp.exp(alog_ref[...])
    la = dt * a
    ri = lax.broadcasted_iota(jnp.int32, (q, q), 0)
    ci = lax.broadcasted_iota(jnp.int32, (q, q), 1)
    causal = ri >= ci
    ltri = jnp.where(causal, 1.0, 0.0).astype(BF16)
    utri = jnp.where(ri <= ci, 1.0, 0.0).astype(BF16)
    parts = _split3(la)
    cum = _dot(ltri, parts[0]) + _dot(ltri, parts[1]) + _dot(ltri, parts[2])
    cum_t = _dot_tn(parts[0], utri) + _dot_tn(parts[1], utri) + _dot_tn(parts[2], utri)
    cum_last = cum[q - 1:q, :]
    dend = jnp.exp(cum_last - cum)
    ecum = jnp.exp(cum)

    stack = jnp.concatenate([dt, dt * dend, ecum], axis=0)
    s_hi, s_lo = _split2(stack)
    cdl = jnp.broadcast_to(jnp.exp(cum_t[:, q - 1:q]), (SSM_HEADS, LANES))
    c_hi, c_lo = _split2(cdl)

    lane = lax.broadcasted_iota(jnp.int32, (q, LANES), 1)
    lo_half = lane < SSM_HEAD_DIM
    causal_o = causal[0:qo]
    cum_o = cum[0:qo]
    gw = SSM_HPG * SSM_HEAD_DIM
    for g in range(SSM_GROUPS):
        gs = slice(g * gw, (g + 1) * gw)
        e_g = e_ref[:, gs]
        exp3 = _dot(s_hi, e_g) + _dot(s_lo, e_g)
        xs_g = act_cur[:, gs]
        xdt = (xs_g * exp3[0:q]).astype(BF16)
        xdd = (xs_g * exp3[q:2 * q]).astype(BF16)
        ecum_e = exp3[2 * q:2 * q + qo]
        bg = act_cur[:, D_INNER + g * D_STATE:D_INNER + (g + 1) * D_STATE].astype(BF16)
        cg = act_cur[:, D_INNER + SSM_GROUPS * D_STATE + g * D_STATE:
                     D_INNER + SSM_GROUPS * D_STATE + (g + 1) * D_STATE].astype(BF16)
        cbm = _dot_nt(cg[0:qo], bg)
        hg = h_scr[gs, :]
        yoff = _dot_nt(cg[0:qo], hg.astype(BF16))
        for pr in range(SSM_HPG // 2):
            h0 = g * SSM_HPG + 2 * pr
            a0 = cbm * jnp.exp(jnp.where(causal_o, cum_o[:, h0:h0 + 1] - cum_t[h0:h0 + 1, :], NEG_INF))
            a1 = cbm * jnp.exp(jnp.where(causal_o, cum_o[:, h0 + 1:h0 + 2] - cum_t[h0 + 1:h0 + 2, :], NEG_INF))
            lhs = jnp.concatenate([a0, a1], axis=1).astype(BF16)
            ps = slice(pr * LANES, (pr + 1) * LANES)
            x2 = xdt[:, ps]
            zero = jnp.zeros_like(x2)
            rhs = jnp.concatenate([jnp.where(lo_half, x2, zero), jnp.where(lo_half, zero, x2)], axis=0)
            yd = _dot(lhs, rhs)
            y_scr[:, g * gw + pr * LANES:g * gw + (pr + 1) * LANES] = yd + yoff[:, ps] * ecum_e[:, ps]
        st = _dot_tn(xdd, bg)
        et_g = et_ref[gs, :]
        cdec = _dot(et_g, c_hi) + _dot(et_g, c_lo)
        h_scr[gs, :] = hg * cdec + st

    y = y_scr[...] + act_cur[0:qo, 0:D_INNER] * dskip_ref[...]
    y = _rms(y * _silu(z_ref[...].astype(F32)), gn_ref[...])
    y_ref[...] = y.astype(y_ref.dtype)

    @pl.when(c == pl.num_programs(1) - 1)
    def _():
        hout_ref[...] = h_scr[...]


def _ssd(xbc, z, dtr, conv_prev, h0, conv_w, conv_b, dt_bias, a_log, d_skip, g_norm):
    b, t, _ = xbc.shape
    q = SSD_Q
    t_valid = min(t, q)
    if t >= q:
        assert t % q == 0
        nc, bt = t // q, q
    else:
        bt = -(-t // SUBLANES) * SUBLANES
        padt = lambda v: jnp.pad(v.astype(F32), ((0, 0), (0, bt - t), (0, 0)))
        xbc, z, dtr = padt(xbc), padt(z), padt(dtr)
        nc = 1
    has_init = h0 is not None
    if not has_init:
        h0 = jnp.zeros((b, SUBLANES, D_STATE), F32)
    hp = SSM_HEADS * SSM_HEAD_DIM
    e = jnp.asarray(np.kron(np.eye(SSM_HEADS, dtype=np.float32), np.ones((1, SSM_HEAD_DIM), np.float32)), BF16)
    et = jnp.asarray(np.kron(np.eye(SSM_HEADS, dtype=np.float32), np.ones((SSM_HEAD_DIM, 1), np.float32)), BF16)
    dskip_e = jnp.repeat(d_skip, SSM_HEAD_DIM).reshape(1, D_INNER)
    kern = functools.partial(_ssd_kernel, q=q, nc=nc, t_valid=t_valid, has_init=has_init)
    h0_block = (None, hp, D_STATE) if has_init else (None, SUBLANES, D_STATE)
    y, h = pl.pallas_call(
        kern,
        grid=(b, nc),
        in_specs=[pl.BlockSpec((None, bt, CONV_DIM), lambda i, c: (i, c, 0)),
                  pl.BlockSpec((None, bt, CONV_DIM), lambda i, c: (i, jnp.minimum(c + 1, nc - 1), 0)),
                  pl.BlockSpec((None, bt, D_INNER), lambda i, c: (i, c, 0)),
                  pl.BlockSpec((None, bt, LANES), lambda i, c: (i, c, 0)),
                  pl.BlockSpec((None, CONV_W - 1, CONV_DIM), lambda i, c: (i, 0, 0)),
                  pl.BlockSpec(h0_block, lambda i, c: (i, 0, 0)),
                  pl.BlockSpec((CONV_W, CONV_DIM), lambda i, c: (0, 0)),
                  _vec_spec(CONV_DIM), _vec_spec(SSM_HEADS), _vec_spec(SSM_HEADS),
                  _vec_spec(D_INNER), _vec_spec(D_INNER),
                  pl.BlockSpec((SSM_HEADS, D_INNER), lambda i, c: (0, 0)),
                  pl.BlockSpec((D_INNER, SSM_HEADS), lambda i, c: (0, 0))],
        out_specs=[pl.BlockSpec((None, bt, D_INNER), lambda i, c: (i, c, 0)),
                   pl.BlockSpec((None, hp, D_STATE), lambda i, c: (i, 0, 0))],
        out_shape=[jax.ShapeDtypeStruct((b, bt * nc, D_INNER), BF16),
                   jax.ShapeDtypeStruct((b, hp, D_STATE), F32)],
        scratch_shapes=[pltpu.VMEM((2 * SUBLANES, CONV_DIM), F32),
                        pltpu.VMEM((min(nc, 2), q, CONV_DIM), F32),
                        pltpu.VMEM((bt, D_INNER), F32),
                        pltpu.VMEM((hp, D_STATE), F32),
                        pltpu.VMEM((q, LANES), F32)],
        compiler_params=_cparams("parallel", "arbitrary"),
        name="ssd",
    )(xbc, xbc, z, dtr, conv_prev, h0, conv_w, conv_b.reshape(1, CONV_DIM), dt_bias.reshape(1, SSM_HEADS),
      a_log.reshape(1, SSM_HEADS), dskip_e, g_norm.reshape(1, D_INNER), e, et)
    return y[:, :t], h


def _attn_prompt_kernel(*refs, slopes, seq):
    qkv_refs = refs[:3 * N_DIL]
    o_ref, og_scr, lg_scr = refs[3 * N_DIL:]
    blk = ATT_BLOCK
    h = pl.program_id(1)
    qi = lax.broadcasted_iota(jnp.int32, (blk, 2 * blk), 0)
    ki = lax.broadcasted_iota(jnp.int32, (blk, 2 * blk), 1)
    dist = qi + blk - ki
    scale = ATT_HEAD_DIM ** -0.5
    for gi, (window, dil) in enumerate(DIL_GROUPS):
        q_ref, k_ref, v_ref = qkv_refs[gi], qkv_refs[N_DIL + gi], qkv_refs[2 * N_DIL + gi]
        slope = slopes[gi][0]
        for hh in range(1, ATT_HPG):
            slope = jnp.where(h == hh, slopes[gi][hh], slope)
        valid2 = (dist >= 0) & (dist <= window // dil)
        bias_any = jnp.where(valid2, -slope * (dil * dist).astype(F32), NEG_INF)
        bias_first = jnp.where(ki >= blk, bias_any, NEG_INF)
        nb = seq // dil // blk

        def rows(start, size):
            return pl.ds(start, size) if dil == 1 else pl.ds(start, size, stride=dil)

        starts = [r + dil * n * blk for r in range(dil) for n in range(nb)]
        firsts = [n == 0 for r in range(dil) for n in range(nb)]

        def band(ref, q0, first):
            if first:
                cur = ref[rows(q0, blk), :].astype(BF16)
                return jnp.concatenate([cur, cur], axis=0)
            return ref[rows(q0 - dil * blk, 2 * blk), :].astype(BF16)

        qs = jnp.stack([q_ref[rows(q0, blk), :].astype(BF16) for q0 in starts])
        ks = jnp.stack([band(k_ref, q0, f) for q0, f in zip(starts, firsts)])
        vs = jnp.stack([band(v_ref, q0, f) for q0, f in zip(starts, firsts)])
        bias = jnp.stack([bias_first if f else bias_any for f in firsts])
        s = jnp.einsum('bqd,bkd->bqk', qs, ks, preferred_element_type=F32) * scale + bias
        m = jnp.max(s, axis=-1, keepdims=True)
        p = jnp.exp(s - m)
        l = jnp.sum(p, axis=-1, keepdims=True)
        o = jnp.einsum('bqk,bkd->bqd', p.astype(BF16), vs, preferred_element_type=F32) / l
        lse = m + jnp.log(l)
        for idx, q0 in enumerate(starts):
            og_scr[gi, rows(q0, blk), :] = o[idx]
            lg_scr[gi, rows(q0, blk), :] = jnp.broadcast_to(lse[idx], (blk, ATT_HEAD_DIM))

    l0, l1, l2 = lg_scr[0], lg_scr[1], lg_scr[2]
    mm = jnp.maximum(jnp.maximum(l0, l1), l2)
    w0, w1, w2 = jnp.exp(l0 - mm), jnp.exp(l1 - mm), jnp.exp(l2 - mm)
    att = (w0 * og_scr[0] + w1 * og_scr[1] + w2 * og_scr[2]) / (w0 + w1 + w2)
    o_ref[...] = att.astype(o_ref.dtype)


def _attn_prompt(qkv):
    b, s, w = qkv.shape
    for window, dil in DIL_GROUPS:
        assert s % (dil * ATT_BLOCK) == 0
    hd = ATT_HEAD_DIM
    specs = [pl.BlockSpec((None, s, hd), functools.partial(lambda i, h, c: (i, 0, c + h), c=_qkv_unit(part, gi) * ATT_HPG))
             for part in range(3) for gi in range(N_DIL)]
    kern = functools.partial(_attn_prompt_kernel, seq=s,
                             slopes=[[float(v) for v in row] for row in _alibi_slopes()])
    return pl.pallas_call(
        kern,
        grid=(b, ATT_HPG),
        in_specs=specs,
        out_specs=pl.BlockSpec((None, s, hd), lambda i, h: (i, 0, h)),
        out_shape=jax.ShapeDtypeStruct((b, s, ATT_OUT_WIDTH), BF16),
        scratch_shapes=[pltpu.VMEM((N_DIL, s, hd), F32), pltpu.VMEM((N_DIL, s, hd), F32)],
        compiler_params=_cparams("parallel", "arbitrary"),
        name="attn_prompt",
    )(*([qkv] * (3 * N_DIL)))


def _attn_sample_kernel(qkv_ref, c0_ref, c1_ref, c2_ref, o_ref, *, t, slopes):
    scale = ATT_HEAD_DIM ** -0.5
    caches = (c0_ref, c1_ref, c2_ref)
    nk = ATT_BLOCK
    step = lax.broadcasted_iota(jnp.int32, (nk, ATT_HPG, 1), 0)
    hid = lax.broadcasted_iota(jnp.int32, (ATT_HPG, 1), 0)
    for tq in range(t):
        outs, lses = [], []
        for gi, (window, dil) in enumerate(DIL_GROUPS):
            kmax = window // dil
            slope = jnp.full((ATT_HPG, 1), slopes[gi][0], F32)
            for hh in range(1, ATT_HPG):
                slope = jnp.where(hid == hh, slopes[gi][hh], slope)
            q = qkv_ref[tq, _qkv_unit(0, gi)]
            cref = caches[gi]
            if dil == 1:
                kc, vc = cref[:, 0], cref[:, 1]
                dist = kmax + tq - step
                new = [(j, tq - j) for j in range(tq + 1)]
            else:
                kc, vc = cref[:, tq, 0], cref[:, tq, 1]
                dist = kmax - step
                new = [(tq, 0)]
            s = jnp.sum(kc * q[None], axis=-1, keepdims=True) * scale - slope[None] * (dil * dist).astype(F32)
            s = jnp.where(dist <= kmax, s, NEG_INF)
            s_new = [jnp.sum(qkv_ref[j, _qkv_unit(1, gi)] * q, axis=-1, keepdims=True) * scale - slope * float(dil * d)
                     for j, d in new]
            m = jnp.max(s, axis=0)
            for sn in s_new:
                m = jnp.maximum(m, sn)
            p = jnp.exp(s - m[None])
            l = jnp.sum(p, axis=0)
            o = jnp.sum(p * vc, axis=0)
            for (j, _), sn in zip(new, s_new):
                pn = jnp.exp(sn - m)
                l = l + pn
                o = o + pn * qkv_ref[j, _qkv_unit(2, gi)]
            outs.append(o / l)
            lses.append(m + jnp.log(l))
        mm = jnp.maximum(jnp.maximum(lses[0], lses[1]), lses[2])
        ws = [jnp.exp(x - mm) for x in lses]
        o_ref[tq] = (ws[0] * outs[0] + ws[1] * outs[1] + ws[2] * outs[2]) / (ws[0] + ws[1] + ws[2])


def _attn_sample(qkv, caches):
    b, t, w = qkv.shape
    qkv5 = qkv.reshape(b, t, 3 * N_DIL, ATT_HPG, ATT_HEAD_DIM)
    specs = [pl.BlockSpec((None, t, 3 * N_DIL, ATT_HPG, ATT_HEAD_DIM), lambda i: (i, 0, 0, 0, 0))]
    views = []
    for (window, dil), cache in zip(DIL_GROUPS, caches):
        lb = cache.shape[1]
        assert lb == window and lb // dil == ATT_BLOCK and (dil == 1 or dil >= t)
        if dil == 1:
            views.append(cache)
            specs.append(pl.BlockSpec((None, lb, 2, ATT_HPG, ATT_HEAD_DIM), lambda i: (i, 0, 0, 0, 0)))
        else:
            views.append(cache.reshape(b, lb // dil, dil, 2, ATT_HPG, ATT_HEAD_DIM))
            specs.append(pl.BlockSpec((None, lb // dil, t, 2, ATT_HPG, ATT_HEAD_DIM), lambda i: (i, 0, 0, 0, 0, 0)))
    kern = functools.partial(_attn_sample_kernel, t=t,
                             slopes=[[float(v) for v in row] for row in _alibi_slopes()])
    att = pl.pallas_call(
        kern,
        grid=(b,),
        in_specs=specs,
        out_specs=pl.BlockSpec((None, t, ATT_HPG, ATT_HEAD_DIM), lambda i: (i, 0, 0, 0)),
        out_shape=jax.ShapeDtypeStruct((b, t, ATT_HPG, ATT_HEAD_DIM), F32),
        compiler_params=_cparams("parallel"),
        name="attn_sample",
    )(qkv5, *views)
    return att.reshape(b, t, ATT_OUT_WIDTH)


def _kv_rows_kernel(k_ref, v_ref, o_ref):
    br = k_ref.shape[0]
    per_row = 2 * ATT_HPG
    for part, ref in enumerate((k_ref, v_ref)):
        for h in range(ATT_HPG):
            o_ref[pl.ds(part * ATT_HPG + h, br, stride=per_row), :] = ref[:, h * ATT_HEAD_DIM:(h + 1) * ATT_HEAD_DIM]


def _kv_rows(qkv, gi, w):
    b, s, _ = qkv.shape
    br = min(w, 512)
    assert w % br == 0 and s % br == 0
    r0 = (s - w) // br
    per_row = 2 * ATT_HPG
    ku, vu = _qkv_unit(1, gi), _qkv_unit(2, gi)
    out = pl.pallas_call(
        _kv_rows_kernel,
        grid=(b, w // br),
        in_specs=[pl.BlockSpec((None, br, ATT_OUT_WIDTH), lambda i, r: (i, r0 + r, ku)),
                  pl.BlockSpec((None, br, ATT_OUT_WIDTH), lambda i, r: (i, r0 + r, vu))],
        out_specs=pl.BlockSpec((None, br * per_row, ATT_HEAD_DIM), lambda i, r: (i, r, 0)),
        out_shape=jax.ShapeDtypeStruct((b, w * per_row, ATT_HEAD_DIM), F32),
        compiler_params=_cparams("parallel", "parallel"),
        name="kv_rows",
    )(qkv, qkv)
    return out.reshape(b, w, 2, ATT_HPG, ATT_HEAD_DIM)


def _merge_kernel(att_ref, y_ref, gs_ref, ga_ref, ws_ref, wa_ref, o_ref):
    bs = _dot(y_ref[...], ws_ref[...])
    ba = _dot(att_ref[...].astype(BF16), wa_ref[...])
    o_ref[...] = (_sigmoid(gs_ref[...].astype(F32)) * bs + _sigmoid(ga_ref[...].astype(F32)) * ba).astype(o_ref.dtype)


def _merge(att, y, gates, w_ssm, w_att, bm):
    m = y.shape[0]
    bn = 512
    nj = D_MODEL // bn
    return pl.pallas_call(
        _merge_kernel,
        grid=(m // bm, nj),
        in_specs=[pl.BlockSpec((bm, ATT_OUT_WIDTH), lambda i, j: (i, 0)),
                  pl.BlockSpec((bm, D_INNER), lambda i, j: (i, 0)),
                  pl.BlockSpec((bm, bn), lambda i, j: (i, j)),
                  pl.BlockSpec((bm, bn), lambda i, j: (i, j + nj)),
                  pl.BlockSpec((D_INNER, bn), lambda i, j: (0, j)),
                  pl.BlockSpec((ATT_OUT_WIDTH, bn), lambda i, j: (0, j))],
        out_specs=pl.BlockSpec((bm, bn), lambda i, j: (i, j)),
        out_shape=jax.ShapeDtypeStruct((m, D_MODEL), BF16),
        compiler_params=_cparams("parallel", "arbitrary"),
        name="merge",
    )(att, y, gates, gates, w_ssm, w_att)


def _outproj_kernel(mg_ref, x_ref, gt_ref, gpost_ref, w_ref, o_ref):
    out = _dot(mg_ref[...], w_ref[...])
    o_ref[...] = x_ref[...] + gt_ref[...] * _rms(out, gpost_ref[...])


def _outproj(merged, x, gate, g_post, w_out, bm, rows_per_batch):
    m, d = x.shape
    return pl.pallas_call(
        _outproj_kernel,
        grid=(m // bm,),
        in_specs=[pl.BlockSpec((bm, d), lambda i: (i, 0)),
                  pl.BlockSpec((bm, d), lambda i: (i, 0)),
                  _mod_spec(gate, bm, rows_per_batch), _vec_spec(d),
                  pl.BlockSpec((d, d), lambda i: (0, 0))],
        out_specs=pl.BlockSpec((bm, d), lambda i: (i, 0)),
        out_shape=jax.ShapeDtypeStruct((m, d), F32),
        compiler_params=_cparams("parallel"),
        name="outproj",
    )(merged, x, gate, g_post.reshape(1, d), w_out)


def _layer(x3, mod, conv_prev, ssm_prev, kv_bufs, p, bm):
    b, t, d = x3.shape
    m = b * t
    prompt = kv_bufs is None
    x = x3.reshape(m, d)
    if prompt:
        mods = [mod[:, k].reshape(b, 1, d) for k in range(3 * N_SUB)]
    else:
        mods = [jnp.repeat(mod[:, k], t, axis=0) for k in range(3 * N_SUB)]

    x = _ffn(x, mods[0], mods[1], mods[2], p['g_pre_ffn1'], p['g_post_ffn1'], p['w_gu_ffn1'],
             p['w_down_ffn1'], bm, t)

    h = _prenorm(x, mods[3], mods[4], p['g_pre_mix'], bm, t)
    bmm = min(m, MM_BM)
    z = _mm(h, p['w_z'], BF16, bmm, MM_BN, "inproj_z")
    xbc = _mm(h, p['w_xbc'], BF16, bmm, MM_BN, "inproj_xbc")
    dtr = _mm(h, p['w_dt'], F32, bmm, LANES, "inproj_dt")
    qkv = _mm(h, p['w_qkv'], F32, bmm, ATT_QKV_WIDTH, "inproj_qkv")
    gates = _mm(h, p['w_gates'], BF16, bmm, MM_BN, "inproj_gates")

    xbc3 = xbc.reshape(b, t, CONV_DIM)
    if prompt:
        conv_prev = jnp.zeros((b, CONV_W - 1, CONV_DIM), F32)
        conv_new = xbc3[:, t - (CONV_W - 1):].astype(F32)
    else:
        conv_new = jnp.concatenate([conv_prev, xbc3.astype(F32)], axis=1)[:, -(CONV_W - 1):]
    h0 = None if prompt else ssm_prev.reshape(b, SSM_HEADS * SSM_HEAD_DIM, D_STATE)
    y, ssm_new = _ssd(xbc3, z.reshape(b, t, D_INNER), dtr.reshape(b, t, LANES), conv_prev, h0,
                      p['conv_w'], p['conv_b'], p['dt_bias'], p['a_log'], p['d_skip'], p['g_ssm_norm'])
    ssm_new = ssm_new.reshape(b, SSM_HEADS, SSM_HEAD_DIM, D_STATE)

    qkv3 = qkv.reshape(b, t, 3 * ATT_QKV_WIDTH)
    kv_new = []
    for gi, (window, dil) in enumerate(DIL_GROUPS):
        c0 = _qkv_unit(1, gi) * ATT_OUT_WIDTH
        rows = slice(t - min(window, t), t) if prompt else slice(0, t)
        if prompt:
            kv_new.append(_kv_rows(qkv3, gi, min(window, t)))
        else:
            new = qkv3[:, rows, c0:c0 + 2 * ATT_OUT_WIDTH].reshape(b, -1, 2, ATT_HPG, ATT_HEAD_DIM)
            lb = kv_bufs[gi].shape[1]
            kv_new.append(jnp.concatenate([kv_bufs[gi], new], axis=1)[:, -lb:])
    if prompt:
        att = _attn_prompt(qkv3)
    else:
        att = _attn_sample(qkv3, kv_bufs)

    merged = _merge(att.reshape(m, ATT_OUT_WIDTH), y.reshape(m, D_INNER), gates, p['w_ssm_proj'],
                    p['w_att_proj'], bmm)
    x = _outproj(merged, x, mods[5], p['g_post_mix'], p['w_out'], bm, t)

    x = _ffn(x, mods[6], mods[7], mods[8], p['g_pre_ffn2'], p['g_post_ffn2'], p['w_gu_ffn2'],
             p['w_down_ffn2'], bm, t)
    return x.reshape(b, t, d), conv_new, ssm_new, kv_new


def _cast_cols_kernel(wt_ref, o_ref):
    o_ref[...] = wt_ref[...].T.astype(o_ref.dtype)


def _cast_cols(wt, start, ncols, bn, out_order=None):
    k = wt.shape[1]
    assert ncols % bn == 0 and start % SUBLANES == 0

    def out_block(j):
        if out_order is None:
            return (0, j)
        dst = out_order[0]
        for src in range(1, len(out_order)):
            dst = jnp.where(j == src, out_order[src], dst)
        return (0, dst)

    return pl.pallas_call(
        _cast_cols_kernel,
        grid=(ncols // bn,),
        in_specs=[pl.BlockSpec((pl.Element(bn), pl.Element(k)), lambda j: (pl.multiple_of(start + j * bn, SUBLANES), 0))],
        out_specs=pl.BlockSpec((k, bn), out_block),
        out_shape=jax.ShapeDtypeStruct((k, ncols), BF16),
        compiler_params=_cparams("parallel"),
        name="cast_cols",
    )(wt)


def _prep_weights(w_gu_ffn1, w_down_ffn1, w_in, w_ssm_proj, w_att_proj, w_out, w_gu_ffn2, w_down_ffn2):
    o = 0
    segs = {}
    qkv_order = [_qkv_unit(part, gi) for part in range(3) for gi in range(N_DIL)]
    w_in_t = w_in.T
    for name, size, bn, order in (('z', D_INNER, 512, None), ('xbc', CONV_DIM, 512, None),
                                  ('dt', SSM_HEADS, LANES, None),
                                  ('qkv', 3 * ATT_QKV_WIDTH, ATT_OUT_WIDTH, qkv_order),
                                  ('gates', 2 * D_MODEL, 512, None)):
        segs[name] = _cast_cols(w_in_t, o, max(size, LANES), bn, order)
        o += size
    return {
        'w_gu_ffn1': w_gu_ffn1.astype(BF16), 'w_down_ffn1': w_down_ffn1.astype(BF16),
        'w_gu_ffn2': w_gu_ffn2.astype(BF16), 'w_down_ffn2': w_down_ffn2.astype(BF16),
        'w_z': segs['z'], 'w_xbc': segs['xbc'], 'w_dt': segs['dt'], 'w_qkv': segs['qkv'],
        'w_gates': segs['gates'],
        'w_ssm_proj': w_ssm_proj.astype(BF16), 'w_att_proj': w_att_proj.astype(BF16),
        'w_out': w_out.astype(BF16),
    }


def kernel(x_prompt, x_sample, c_prompt, c_sample, state_ssm, state_conv, cache_kv_w128, cache_kv_w512, cache_kv_w2048, w_ada, b_ada, g_pre_ffn1, g_post_ffn1, w_gu_ffn1, w_down_ffn1, g_pre_mix, g_post_mix, w_in, conv_w, conv_b, dt_bias, a_log, d_skip, g_ssm_norm, w_ssm_proj, w_att_proj, w_out, g_pre_ffn2, g_post_ffn2, w_gu_ffn2, w_down_ffn2):
    depth = w_ada.shape[0]
    kv_in = (cache_kv_w128, cache_kv_w512, cache_kv_w2048)
    bp = x_prompt.shape[0]
    yp, ys = x_prompt, x_sample
    outs = [[] for _ in range(10)]
    for l in range(depth):
        p = _prep_weights(w_gu_ffn1[l], w_down_ffn1[l], w_in[l], w_ssm_proj[l], w_att_proj[l], w_out[l],
                          w_gu_ffn2[l], w_down_ffn2[l])
        p.update({'g_pre_ffn1': g_pre_ffn1[l], 'g_post_ffn1': g_post_ffn1[l], 'g_pre_mix': g_pre_mix[l],
                  'g_post_mix': g_post_mix[l], 'g_pre_ffn2': g_pre_ffn2[l], 'g_post_ffn2': g_post_ffn2[l],
                  'conv_w': conv_w[l], 'conv_b': conv_b[l], 'dt_bias': dt_bias[l], 'a_log': a_log[l],
                  'd_skip': d_skip[l], 'g_ssm_norm': g_ssm_norm[l]})
        c_all = jnp.concatenate([c_prompt, c_sample], axis=0)
        mod = _ada(c_all, w_ada[l], b_ada[l]).reshape(c_all.shape[0], 3 * N_SUB, D_MODEL)
        yp, cp, sp, kp = _layer(yp, mod[:bp], None, None, None, p, 512)
        ys, cs, ss, ksm = _layer(ys, mod[bp:], state_conv[l], state_ssm[l], tuple(kv[l] for kv in kv_in), p,
                                 ys.shape[0] * ys.shape[1])
        for lst, val in zip(outs, (sp, ss, cp, cs, kp[0], ksm[0], kp[1], ksm[1], kp[2], ksm[2])):
            lst.append(val)
    return (yp, ys) + tuple(jnp.stack(o) for o in outs)
```

```python
import functools
import math

import numpy as np
import jax
import jax.numpy as jnp
from jax import lax
from jax.experimental import pallas as pl
from jax.experimental.pallas import tpu as pltpu

D_MODEL = 2048
SSM_HEAD_DIM = 64
D_INNER = 2 * D_MODEL
SSM_HEADS = D_INNER // SSM_HEAD_DIM
SSM_GROUPS = 8
SSM_HPG = SSM_HEADS // SSM_GROUPS
D_STATE = 128
CONV_W = 4
CONV_DIM = D_INNER + 2 * SSM_GROUPS * D_STATE
ATT_HEAD_DIM = 128
DIL_GROUPS = ((128, 1), (512, 4), (2048, 16))
N_DIL = len(DIL_GROUPS)
ATT_HPG = 4
ATT_HEADS = N_DIL * ATT_HPG
ATT_QKV_WIDTH = ATT_HEADS * ATT_HEAD_DIM
ATT_OUT_WIDTH = ATT_HPG * ATT_HEAD_DIM
ATT_BLOCK = 128
ALIBI_MAX_BIAS = 8.0
D_FF = ((8 * D_MODEL // 3 + 255) // 256) * 256
N_SUB = 3
EPS = 1e-6
NEG_INF = -1e30

LANES = 128
SUBLANES = 8
VMEM_LIMIT = 56 * 1024 * 1024
SSD_Q = 128
MM_BM = 1024
MM_BN = 1024

F32 = jnp.float32
BF16 = jnp.bfloat16


def _cparams(*sem):
    return pltpu.CompilerParams(dimension_semantics=sem, vmem_limit_bytes=VMEM_LIMIT)


def _qkv_unit(part, gi):
    return gi if part == 0 else N_DIL + 2 * gi + (part - 1)


def _alibi_slopes():
    i = np.arange(1, ATT_HEADS + 1, dtype=np.float32)
    return np.exp2(-ALIBI_MAX_BIAS * i / ATT_HEADS).astype(np.float32).reshape(N_DIL, ATT_HPG)


def _silu(x):
    return x * (1.0 / (1.0 + jnp.exp(-x)))


def _sigmoid(x):
    return 1.0 / (1.0 + jnp.exp(-x))


def _rms(x, g):
    return x * lax.rsqrt(jnp.mean(x * x, axis=-1, keepdims=True) + EPS) * g


def _split2(x):
    hi = x.astype(BF16)
    lo = (x - hi.astype(F32)).astype(BF16)
    return hi, lo


def _split3(x):
    hi = x.astype(BF16)
    r = x - hi.astype(F32)
    mid = r.astype(BF16)
    lo = (r - mid.astype(F32)).astype(BF16)
    return hi, mid, lo


def _dot(a, b):
    return jnp.dot(a, b, preferred_element_type=F32)


def _dot_nt(a, b):
    return lax.dot_general(a, b, (((1,), (1,)), ((), ())), preferred_element_type=F32)


def _dot_tn(a, b):
    return lax.dot_general(a, b, (((0,), (0,)), ((), ())), preferred_element_type=F32)


def _mod_spec(mod, bm, rows_per_batch):
    if mod.ndim == 3:
        bpb = rows_per_batch // bm
        return pl.BlockSpec((None, 1, D_MODEL), lambda i, *_: (i // bpb, 0, 0))
    return pl.BlockSpec((bm, D_MODEL), lambda i, *_: (i, 0))


def _vec_spec(n):
    return pl.BlockSpec((1, n), lambda *_: (0, 0))


def _ada_kernel(c_ref, w_ref, b_ref, o_ref):
    c = _silu(c_ref[...]).astype(BF16)
    o_ref[...] = _dot(c, w_ref[...].astype(BF16)) + b_ref[...]


def _ada(c, w, b):
    m, k = c.shape
    n = w.shape[1]
    bn = 1024
    return pl.pallas_call(
        _ada_kernel,
        grid=(n // bn,),
        in_specs=[pl.BlockSpec((m, k), lambda j: (0, 0)),
                  pl.BlockSpec((k, bn), lambda j: (0, j)),
                  pl.BlockSpec((1, bn), lambda j: (0, j))],
        out_specs=pl.BlockSpec((m, bn), lambda j: (0, j)),
        out_shape=jax.ShapeDtypeStruct((m, n), F32),
        compiler_params=_cparams("parallel"),
        name="ada",
    )(c, w, b.reshape(1, n))


KV_SHIFT_PARTS = 4
KV_SHIFT_EVERY = 2


def _kv_shift_step(n, old_refs, new_refs, out_refs, bulk_sem, new_sem):
    nb = old_refs[0].shape[0]
    n_chunks = nb * KV_SHIFT_PARTS
    tick = n % KV_SHIFT_EVERY == 0
    k = n // KV_SHIFT_EVERY

    def bulk(c, kk):
        rows_new = new_refs[c].shape[1]
        rows = (old_refs[c].shape[1] - rows_new) // KV_SHIFT_PARTS
        bi = kk // KV_SHIFT_PARTS
        r0 = pl.multiple_of((kk % KV_SHIFT_PARTS) * rows, SUBLANES)
        return pltpu.make_async_copy(old_refs[c].at[bi, pl.ds(r0 + rows_new, rows), :],
                                     out_refs[c].at[bi, pl.ds(r0, rows), :], bulk_sem.at[kk % 2, c])

    def tail(c):
        rows_new = new_refs[c].shape[1]
        keep = old_refs[c].shape[1] - rows_new
        return pltpu.make_async_copy(new_refs[c], out_refs[c].at[:, pl.ds(keep, rows_new), :], new_sem.at[c])

    @pl.when(n == 0)
    def _():
        for c in range(len(old_refs)):
            tail(c).start()

    @pl.when(tick & (k < n_chunks))
    def _():
        for c in range(len(old_refs)):
            bulk(c, k).start()

    @pl.when(tick & (k >= 1) & (k <= n_chunks))
    def _():
        for c in range(len(old_refs)):
            bulk(c, k - 1).wait()

    @pl.when(n == KV_SHIFT_EVERY * n_chunks)
    def _():
        for c in range(len(old_refs)):
            tail(c).wait()


def _ffn_kernel(*refs, n_side):
    x_ref, sh_ref, sc_ref, gt_ref, gpre_ref, gpost_ref, wg_ref, wu_ref, wd_ref = refs[:9]
    side_in = refs[9:9 + 2 * n_side]
    o_ref = refs[9 + 2 * n_side]
    side_out = refs[10 + 2 * n_side:10 + 3 * n_side]
    h_scr, acc_scr = refs[10 + 3 * n_side:12 + 3 * n_side]
    j = pl.program_id(1)
    if n_side:
        bulk_sem, new_sem = refs[12 + 3 * n_side:]
        _kv_shift_step(pl.program_id(0) * pl.num_programs(1) + j, side_in[:n_side], side_in[n_side:], side_out,
                       bulk_sem, new_sem)

    @pl.when(j == 0)
    def _():
        x = x_ref[...]
        h = _rms(x, gpre_ref[...]) * (1.0 + sc_ref[...]) + sh_ref[...]
        h_scr[...] = h.astype(BF16)
        acc_scr[...] = jnp.zeros_like(acc_scr)

    h = h_scr[...]
    g = _dot(h, wg_ref[...])
    u = _dot(h, wu_ref[...])
    a = (_silu(g) * u).astype(BF16)
    acc_scr[...] += _dot(a, wd_ref[...])

    @pl.when(j == pl.num_programs(1) - 1)
    def _():
        o_ref[...] = x_ref[...] + 0.5 * gt_ref[...] * _rms(acc_scr[...], gpost_ref[...])


def _ffn(x, shift, scale, gate, g_pre, g_post, w_gu, w_down, bm, rows_per_batch, kv_shift=None):
    m, d = x.shape
    ff = w_down.shape[0]
    bf = 512
    nj = ff // bf
    ms = lambda a: _mod_spec(a, bm, rows_per_batch)
    olds, news = kv_shift if kv_shift is not None else ([], [])
    n_side = len(olds)
    if n_side:
        assert (m // bm) * nj > KV_SHIFT_EVERY * olds[0].shape[0] * KV_SHIFT_PARTS
        for o, nw in zip(olds, news):
            assert (o.shape[1] - nw.shape[1]) % (KV_SHIFT_PARTS * SUBLANES) == 0 and nw.shape[1] % SUBLANES == 0
    any_spec = pl.BlockSpec(memory_space=pl.ANY)
    out = pl.pallas_call(
        functools.partial(_ffn_kernel, n_side=n_side),
        grid=(m // bm, nj),
        in_specs=[pl.BlockSpec((bm, d), lambda i, j: (i, 0)),
                  ms(shift), ms(scale), ms(gate), _vec_spec(d), _vec_spec(d),
                  pl.BlockSpec((d, bf), lambda i, j: (0, j)),
                  pl.BlockSpec((d, bf), lambda i, j: (0, j + nj)),
                  pl.BlockSpec((bf, d), lambda i, j: (j, 0))] + [any_spec] * (2 * n_side),
        out_specs=[pl.BlockSpec((bm, d), lambda i, j: (i, 0))] + [any_spec] * n_side,
        out_shape=[jax.ShapeDtypeStruct((m, d), F32)] + [jax.ShapeDtypeStruct(o.shape, o.dtype) for o in olds],
        scratch_shapes=[pltpu.VMEM((bm, d), BF16), pltpu.VMEM((bm, d), F32)] + (
            [pltpu.SemaphoreType.DMA((2, n_side)), pltpu.SemaphoreType.DMA((n_side,))] if n_side else []),
        compiler_params=_cparams("arbitrary" if n_side else "parallel", "arbitrary"),
        name="ffn",
    )(x, shift, scale, gate, g_pre.reshape(1, d), g_post.reshape(1, d), w_gu, w_gu, w_down, *olds, *news)
    return (out[0], out[1:]) if n_side else out[0]


def _prenorm_kernel(x_ref, sh_ref, sc_ref, g_ref, o_ref):
    o_ref[...] = (_rms(x_ref[...], g_ref[...]) * (1.0 + sc_ref[...]) + sh_ref[...]).astype(o_ref.dtype)


def _prenorm(x, shift, scale, g, bm, rows_per_batch):
    m, d = x.shape
    ms = lambda a: _mod_spec(a, bm, rows_per_batch)
    return pl.pallas_call(
        _prenorm_kernel,
        grid=(m // bm,),
        in_specs=[pl.BlockSpec((bm, d), lambda i: (i, 0)), ms(shift), ms(scale), _vec_spec(d)],
        out_specs=pl.BlockSpec((bm, d), lambda i: (i, 0)),
        out_shape=jax.ShapeDtypeStruct((m, d), BF16),
        compiler_params=_cparams("parallel"),
        name="prenorm",
    )(x, shift, scale, g.reshape(1, d))


def _mm_kernel(x_ref, w_ref, o_ref):
    o_ref[...] = _dot(x_ref[...], w_ref[...]).astype(o_ref.dtype)


def _mm(x, w, out_dtype, bm, bn, name):
    m, k = x.shape
    n = w.shape[1]
    return pl.pallas_call(
        _mm_kernel,
        grid=(m // bm, n // bn),
        in_specs=[pl.BlockSpec((bm, k), lambda i, j: (i, 0)),
                  pl.BlockSpec((k, bn), lambda i, j: (0, j))],
        out_specs=pl.BlockSpec((bm, bn), lambda i, j: (i, j)),
        out_shape=jax.ShapeDtypeStruct((m, n), out_dtype),
        compiler_params=_cparams("parallel", "arbitrary"),
        name=name,
    )(x, w)


def _ssd_kernel(xbc_ref, xbcn_ref, z_ref, dtr_ref, prev_ref, h0_ref, cw_ref, cb_ref, dtb_ref, alog_ref,
                dskip_ref, gn_ref, e_ref, et_ref, y_ref, hout_ref,
                xp_scr, act_scr, y_scr, h_scr, dt_scr, *, q, nc, t_valid, has_init):
    c = pl.program_id(1)
    w8 = SUBLANES
    qo = xbc_ref.shape[0]
    slab = 512

    def rows8(ref, start):
        if ref.dtype == F32:
            return ref[start:start + w8, :]
        base = start // (2 * w8) * (2 * w8)
        return ref[base:base + 2 * w8, :].astype(F32)[start - base:start - base + w8]

    def conv_chunk(src_ref, act):
        if qo < q:
            assert qo == w8
            act[...] = jnp.zeros((q, CONV_DIM), F32)
        else:
            ri = lax.broadcasted_iota(jnp.int32, (q, q), 0)
            ci = lax.broadcasted_iota(jnp.int32, (q, q), 1)
            shift = jnp.concatenate([jnp.where(ri - ci == CONV_W - 1 - i, 1.0, 0.0) for i in range(CONV_W - 1)],
                                    axis=0).astype(BF16)
            for s in range(CONV_DIM // slab):
                cs = slice(s * slab, (s + 1) * slab)
                xb = src_ref[:, cs]
                sh = _dot(shift, xb)
                acc = cb_ref[:, cs] + cw_ref[CONV_W - 1:CONV_W, cs] * xb.astype(F32)
                for i in range(CONV_W - 1):
                    acc = acc + cw_ref[i:i + 1, cs] * sh[i * q:(i + 1) * q]
                act[:, cs] = _silu(acc)
        xp_scr[w8:2 * w8, :] = rows8(src_ref, 0)
        for s in range(CONV_DIM // slab):
            cs = slice(s * slab, (s + 1) * slab)
            acc = cb_ref[:, cs]
            for i in range(CONV_W):
                off = w8 - (CONV_W - 1) + i
                acc = acc + cw_ref[i:i + 1, cs] * xp_scr[off:off + w8, cs]
            act[0:w8, cs] = _silu(acc)
        xp_scr[0:w8, :] = rows8(src_ref, qo - w8)

    @pl.when(c == 0)
    def _():
        xp_scr[0:w8, :] = jnp.zeros((w8, CONV_DIM), F32)
        xp_scr[w8 - (CONV_W - 1):w8, :] = prev_ref[...]
        if has_init:
            h_scr[...] = h0_ref[...]
        else:
            h_scr[...] = jnp.zeros_like(h_scr)
        conv_chunk(xbc_ref, act_scr.at[0])

    if nc > 1:
        slot = c % 2
        conv_chunk(xbcn_ref, act_scr.at[1 - slot])
        act_cur = act_scr.at[slot]
    else:
        act_cur = act_scr.at[0]

    if qo < q:
        dt_scr[...] = jnp.zeros_like(dt_scr)
        dt_scr[0:qo, :] = dtr_ref[...]
    else:
        dt_scr[...] = dtr_ref[...]

    x = dt_scr[:, 0:SSM_HEADS] + dtb_ref[...]
    dt = jnp.maximum(x, 0.0) + jnp.log1p(jnp.exp(-jnp.abs(x)))
    if t_valid < q:
        rows = lax.broadcasted_iota(jnp.int32, (q, SSM_HEADS), 0)
        dt = jnp.where(rows < t_valid, dt, 0.0)
    a = -jnp.exp(alog_ref[...])
    la = dt * a
    ri = lax.broadcasted_iota(jnp.int32, (q, q), 0)
    ci = lax.broadcasted_iota(jnp.int32, (q, q), 1)
    causal = ri >= ci
    ltri = jnp.where(causal, 1.0, 0.0).astype(BF16)
    utri = jnp.where(ri <= ci, 1.0, 0.0).astype(BF16)
    parts = _split3(la)
    cum = _dot(ltri, parts[0]) + _dot(ltri, parts[1]) + _dot(ltri, parts[2])
    cum_t = _dot_tn(parts[0], utri) + _dot_tn(parts[1], utri) + _dot_tn(parts[2], utri)
    cum_last = cum[q - 1:q, :]
    dend = jnp.exp(cum_last - cum)
    ecum = jnp.exp(cum)

    stack = jnp.concatenate([dt, dt * dend, ecum], axis=0)
    s_hi, s_lo = _split2(stack)
    cdl = jnp.broadcast_to(jnp.exp(cum_t[:, q - 1:q]), (SSM_HEADS, LANES))
    c_hi, c_lo = _split2(cdl)

    lane = lax.broadcasted_iota(jnp.int32, (q, LANES), 1)
    lo_half = lane < SSM_HEAD_DIM
    causal_o = causal[0:qo]
    cum_o = cum[0:qo]
    gw = SSM_HPG * SSM_HEAD_DIM
    for g in range(SSM_GROUPS):
        gs = slice(g * gw, (g + 1) * gw)
        e_g = e_ref[:, gs]
        exp3 = _dot(s_hi, e_g) + _dot(s_lo, e_g)
        xs_g = act_cur[:, gs]
        xdt = (xs_g * exp3[0:q]).astype(BF16)
        xdd = (xs_g * exp3[q:2 * q]).astype(BF16)
        ecum_e = exp3[2 * q:2 * q + qo]
        bg = act_cur[:, D_INNER + g * D_STATE:D_INNER + (g + 1) * D_STATE].astype(BF16)
        cg = act_cur[:, D_INNER + SSM_GROUPS * D_STATE + g * D_STATE:
                     D_INNER + SSM_GROUPS * D_STATE + (g + 1) * D_STATE].astype(BF16)
        cbm = _dot_nt(cg[0:qo], bg)
        hg = h_scr[gs, :]
        yoff = _dot_nt(cg[0:qo], hg.astype(BF16))
        for pr in range(SSM_HPG // 2):
            h0 = g * SSM_HPG + 2 * pr
            a0 = cbm * jnp.exp(jnp.where(causal_o, cum_o[:, h0:h0 + 1] - cum_t[h0:h0 + 1, :], NEG_INF))
            a1 = cbm * jnp.exp(jnp.where(causal_o, cum_o[:, h0 + 1:h0 + 2] - cum_t[h0 + 1:h0 + 2, :], NEG_INF))
            lhs = jnp.concatenate([a0, a1], axis=1).astype(BF16)
            ps = slice(pr * LANES, (pr + 1) * LANES)
            x2 = xdt[:, ps]
            zero = jnp.zeros_like(x2)
            rhs = jnp.concatenate([jnp.where(lo_half, x2, zero), jnp.where(lo_half, zero, x2)], axis=0)
            yd = _dot(lhs, rhs)
            y_scr[:, g * gw + pr * LANES:g * gw + (pr + 1) * LANES] = yd + yoff[:, ps] * ecum_e[:, ps]
        st = _dot_tn(xdd, bg)
        et_g = et_ref[gs, :]
        cdec = _dot(et_g, c_hi) + _dot(et_g, c_lo)
        h_scr[gs, :] = hg * cdec + st

    y = y_scr[...] + act_cur[0:qo, 0:D_INNER] * dskip_ref[...]
    y = _rms(y * _silu(z_ref[...].astype(F32)), gn_ref[...])
    y_ref[...] = y.astype(y_ref.dtype)

    @pl.when(c == pl.num_programs(1) - 1)
    def _():
        hout_ref[...] = h_scr[...]


def _ssd(xbc, z, dtr, conv_prev, h0, conv_w, conv_b, dt_bias, a_log, d_skip, g_norm):
    b, t, _ = xbc.shape
    q = SSD_Q
    t_valid = min(t, q)
    if t >= q:
        assert t % q == 0
        nc, bt = t // q, q
    else:
        bt = -(-t // SUBLANES) * SUBLANES
        padt = lambda v: jnp.pad(v.astype(F32), ((0, 0), (0, bt - t), (0, 0)))
        xbc, z, dtr = padt(xbc), padt(z), padt(dtr)
        nc = 1
    has_init = h0 is not None
    if not has_init:
        h0 = jnp.zeros((b, SUBLANES, D_STATE), F32)
    hp = SSM_HEADS * SSM_HEAD_DIM
    e = jnp.asarray(np.kron(np.eye(SSM_HEADS, dtype=np.float32), np.ones((1, SSM_HEAD_DIM), np.float32)), BF16)
    et = jnp.asarray(np.kron(np.eye(SSM_HEADS, dtype=np.float32), np.ones((SSM_HEAD_DIM, 1), np.float32)), BF16)
    dskip_e = jnp.repeat(d_skip, SSM_HEAD_DIM).reshape(1, D_INNER)
    kern = functools.partial(_ssd_kernel, q=q, nc=nc, t_valid=t_valid, has_init=has_init)
    h0_block = (None, hp, D_STATE) if has_init else (None, SUBLANES, D_STATE)
    y, h = pl.pallas_call(
        kern,
        grid=(b, nc),
        in_specs=[pl.BlockSpec((None, bt, CONV_DIM), lambda i, c: (i, c, 0)),
                  pl.BlockSpec((None, bt, CONV_DIM), lambda i, c: (i, jnp.minimum(c + 1, nc - 1), 0)),
                  pl.BlockSpec((None, bt, D_INNER), lambda i, c: (i, c, 0)),
                  pl.BlockSpec((None, bt, LANES), lambda i, c: (i, c, 0)),
                  pl.BlockSpec((None, CONV_W - 1, CONV_DIM), lambda i, c: (i, 0, 0)),
                  pl.BlockSpec(h0_block, lambda i, c: (i, 0, 0)),
                  pl.BlockSpec((CONV_W, CONV_DIM), lambda i, c: (0, 0)),
                  _vec_spec(CONV_DIM), _vec_spec(SSM_HEADS), _vec_spec(SSM_HEADS),
                  _vec_spec(D_INNER), _vec_spec(D_INNER),
                  pl.BlockSpec((SSM_HEADS, D_INNER), lambda i, c: (0, 0)),
                  pl.BlockSpec((D_INNER, SSM_HEADS), lambda i, c: (0, 0))],
        out_specs=[pl.BlockSpec((None, bt, D_INNER), lambda i, c: (i, c, 0)),
                   pl.BlockSpec((None, hp, D_STATE), lambda i, c: (i, 0, 0))],
        out_shape=[jax.ShapeDtypeStruct((b, bt * nc, D_INNER), BF16),
                   jax.ShapeDtypeStruct((b, hp, D_STATE), F32)],
        scratch_shapes=[pltpu.VMEM((2 * SUBLANES, CONV_DIM), F32),
                        pltpu.VMEM((min(nc, 2), q, CONV_DIM), F32),
                        pltpu.VMEM((bt, D_INNER), F32),
                        pltpu.VMEM((hp, D_STATE), F32),
                        pltpu.VMEM((q, LANES), F32)],
        compiler_params=_cparams("parallel", "arbitrary"),
        name="ssd",
    )(xbc, xbc, z, dtr, conv_prev, h0, conv_w, conv_b.reshape(1, CONV_DIM), dt_bias.reshape(1, SSM_HEADS),
      a_log.reshape(1, SSM_HEADS), dskip_e, g_norm.reshape(1, D_INNER), e, et)
    return y[:, :t], h


def _attn_prompt_kernel(*refs, slopes, seq):
    qkv_refs = refs[:3 * N_DIL]
    o_ref, og_scr, lg_scr = refs[3 * N_DIL:]
    blk = ATT_BLOCK
    h = pl.program_id(1)
    qi = lax.broadcasted_iota(jnp.int32, (blk, 2 * blk), 0)
    ki = lax.broadcasted_iota(jnp.int32, (blk, 2 * blk), 1)
    dist = qi + blk - ki
    scale = ATT_HEAD_DIM ** -0.5
    for gi, (window, dil) in enumerate(DIL_GROUPS):
        q_ref, k_ref, v_ref = qkv_refs[gi], qkv_refs[N_DIL + gi], qkv_refs[2 * N_DIL + gi]
        slope = slopes[gi][0]
        for hh in range(1, ATT_HPG):
            slope = jnp.where(h == hh, slopes[gi][hh], slope)
        valid2 = (dist >= 0) & (dist <= window // dil)
        bias_any = jnp.where(valid2, -slope * (dil * dist).astype(F32), NEG_INF)
        bias_first = jnp.where(ki >= blk, bias_any, NEG_INF)
        nb = seq // dil // blk

        def rows(start, size):
            return pl.ds(start, size) if dil == 1 else pl.ds(start, size, stride=dil)

        starts = [r + dil * n * blk for r in range(dil) for n in range(nb)]
        firsts = [n == 0 for r in range(dil) for n in range(nb)]

        def band(ref, q0, first):
            if first:
                cur = ref[rows(q0, blk), :].astype(BF16)
                return jnp.concatenate([cur, cur], axis=0)
            return ref[rows(q0 - dil * blk, 2 * blk), :].astype(BF16)

        qs = jnp.stack([q_ref[rows(q0, blk), :].astype(BF16) for q0 in starts])
        ks = jnp.stack([band(k_ref, q0, f) for q0, f in zip(starts, firsts)])
        vs = jnp.stack([band(v_ref, q0, f) for q0, f in zip(starts, firsts)])
        bias = jnp.stack([bias_first if f else bias_any for f in firsts])
        s = jnp.einsum('bqd,bkd->bqk', qs, ks, preferred_element_type=F32) * scale + bias
        m = jnp.max(s, axis=-1, keepdims=True)
        p = jnp.exp(s - m)
        l = jnp.sum(p, axis=-1, keepdims=True)
        o = jnp.einsum('bqk,bkd->bqd', p.astype(BF16), vs, preferred_element_type=F32) / l
        lse = m + jnp.log(l)
        for idx, q0 in enumerate(starts):
            og_scr[gi, rows(q0, blk), :] = o[idx]
            lg_scr[gi, rows(q0, blk), :] = jnp.broadcast_to(lse[idx], (blk, ATT_HEAD_DIM))

    l0, l1, l2 = lg_scr[0], lg_scr[1], lg_scr[2]
    mm = jnp.maximum(jnp.maximum(l0, l1), l2)
    w0, w1, w2 = jnp.exp(l0 - mm), jnp.exp(l1 - mm), jnp.exp(l2 - mm)
    att = (w0 * og_scr[0] + w1 * og_scr[1] + w2 * og_scr[2]) / (w0 + w1 + w2)
    o_ref[...] = att.astype(o_ref.dtype)


def _attn_prompt(qkv):
    b, s, w = qkv.shape
    for window, dil in DIL_GROUPS:
        assert s % (dil * ATT_BLOCK) == 0
    hd = ATT_HEAD_DIM
    specs = [pl.BlockSpec((None, s, hd), functools.partial(lambda i, h, c: (i, 0, c + h), c=_qkv_unit(part, gi) * ATT_HPG))
             for part in range(3) for gi in range(N_DIL)]
    kern = functools.partial(_attn_prompt_kernel, seq=s,
                             slopes=[[float(v) for v in row] for row in _alibi_slopes()])
    return pl.pallas_call(
        kern,
        grid=(b, ATT_HPG),
        in_specs=specs,
        out_specs=pl.BlockSpec((None, s, hd), lambda i, h: (i, 0, h)),
        out_shape=jax.ShapeDtypeStruct((b, s, ATT_OUT_WIDTH), BF16),
        scratch_shapes=[pltpu.VMEM((N_DIL, s, hd), F32), pltpu.VMEM((N_DIL, s, hd), F32)],
        compiler_params=_cparams("parallel", "arbitrary"),
        name="attn_prompt",
    )(*([qkv] * (3 * N_DIL)))


def _attn_sample_kernel(qkv_ref, c0_ref, c1_ref, c2_ref, o_ref, *, t, slopes):
    scale = ATT_HEAD_DIM ** -0.5
    caches = (c0_ref, c1_ref, c2_ref)
    nk = ATT_BLOCK
    step = lax.broadcasted_iota(jnp.int32, (nk, ATT_HPG, 1), 0)
    hid = lax.broadcasted_iota(jnp.int32, (ATT_HPG, 1), 0)
    for tq in range(t):
        outs, lses = [], []
        for gi, (window, dil) in enumerate(DIL_GROUPS):
            kmax = window // dil
            slope = jnp.full((ATT_HPG, 1), slopes[gi][0], F32)
            for hh in range(1, ATT_HPG):
                slope = jnp.where(hid == hh, slopes[gi][hh], slope)
            q = qkv_ref[tq, _qkv_unit(0, gi)]
            cref = caches[gi]
            if dil == 1:
                kc, vc = cref[:, 0], cref[:, 1]
                dist = kmax + tq - step
                new = [(j, tq - j) for j in range(tq + 1)]
            else:
                kc, vc = cref[:, tq, 0], cref[:, tq, 1]
                dist = kmax - step
                new = [(tq, 0)]
            s = jnp.sum(kc * q[None], axis=-1, keepdims=True) * scale - slope[None] * (dil * dist).astype(F32)
            s = jnp.where(dist <= kmax, s, NEG_INF)
            s_new = [jnp.sum(qkv_ref[j, _qkv_unit(1, gi)] * q, axis=-1, keepdims=True) * scale - slope * float(dil * d)
                     for j, d in new]
            m = jnp.max(s, axis=0)
            for sn in s_new:
                m = jnp.maximum(m, sn)
            p = jnp.exp(s - m[None])
            l = jnp.sum(p, axis=0)
            o = jnp.sum(p * vc, axis=0)
            for (j, _), sn in zip(new, s_new):
                pn = jnp.exp(sn - m)
                l = l + pn
                o = o + pn * qkv_ref[j, _qkv_unit(2, gi)]
            outs.append(o / l)
            lses.append(m + jnp.log(l))
        mm = jnp.maximum(jnp.maximum(lses[0], lses[1]), lses[2])
        ws = [jnp.exp(x - mm) for x in lses]
        o_ref[tq] = (ws[0] * outs[0] + ws[1] * outs[1] + ws[2] * outs[2]) / (ws[0] + ws[1] + ws[2])


def _attn_sample(qkv, caches):
    b, t, w = qkv.shape
    qkv5 = qkv.reshape(b, t, 3 * N_DIL, ATT_HPG, ATT_HEAD_DIM)
    specs = [pl.BlockSpec((None, t, 3 * N_DIL, ATT_HPG, ATT_HEAD_DIM), lambda i: (i, 0, 0, 0, 0))]
    views = []
    for (window, dil), cache in zip(DIL_GROUPS, caches):
        lb = cache.shape[1]
        assert lb == window and lb // dil == ATT_BLOCK and (dil == 1 or dil >= t)
        if dil == 1:
            views.append(cache)
            specs.append(pl.BlockSpec((None, lb, 2, ATT_HPG, ATT_HEAD_DIM), lambda i: (i, 0, 0, 0, 0)))
        else:
            views.append(cache.reshape(b, lb // dil, dil, 2, ATT_HPG, ATT_HEAD_DIM))
            specs.append(pl.BlockSpec((None, lb // dil, t, 2, ATT_HPG, ATT_HEAD_DIM), lambda i: (i, 0, 0, 0, 0, 0)))
    kern = functools.partial(_attn_sample_kernel, t=t,
                             slopes=[[float(v) for v in row] for row in _alibi_slopes()])
    att = pl.pallas_call(
        kern,
        grid=(b,),
        in_specs=specs,
        out_specs=pl.BlockSpec((None, t, ATT_HPG, ATT_HEAD_DIM), lambda i: (i, 0, 0, 0)),
        out_shape=jax.ShapeDtypeStruct((b, t, ATT_HPG, ATT_HEAD_DIM), F32),
        compiler_params=_cparams("parallel"),
        name="attn_sample",
    )(qkv5, *views)
    return att.reshape(b, t, ATT_OUT_WIDTH)


def _kv_rows_kernel(k_ref, v_ref, o_ref):
    br = k_ref.shape[0]
    per_row = 2 * ATT_HPG
    for part, ref in enumerate((k_ref, v_ref)):
        for h in range(ATT_HPG):
            o_ref[pl.ds(part * ATT_HPG + h, br, stride=per_row), :] = ref[:, h * ATT_HEAD_DIM:(h + 1) * ATT_HEAD_DIM]


def _kv_rows(qkv, gi, w):
    b, s, _ = qkv.shape
    br = min(w, 512)
    assert w % br == 0 and s % br == 0
    r0 = (s - w) // br
    per_row = 2 * ATT_HPG
    ku, vu = _qkv_unit(1, gi), _qkv_unit(2, gi)
    out = pl.pallas_call(
        _kv_rows_kernel,
        grid=(b, w // br),
        in_specs=[pl.BlockSpec((None, br, ATT_OUT_WIDTH), lambda i, r: (i, r0 + r, ku)),
                  pl.BlockSpec((None, br, ATT_OUT_WIDTH), lambda i, r: (i, r0 + r, vu))],
        out_specs=pl.BlockSpec((None, br * per_row, ATT_HEAD_DIM), lambda i, r: (i, r, 0)),
        out_shape=jax.ShapeDtypeStruct((b, w * per_row, ATT_HEAD_DIM), F32),
        compiler_params=_cparams("parallel", "parallel"),
        name="kv_rows",
    )(qkv, qkv)
    return out.reshape(b, w, 2, ATT_HPG, ATT_HEAD_DIM)


def _merge_kernel(att_ref, y_ref, gs_ref, ga_ref, ws_ref, wa_ref, o_ref):
    bs = _dot(y_ref[...], ws_ref[...])
    ba = _dot(att_ref[...].astype(BF16), wa_ref[...])
    o_ref[...] = (_sigmoid(gs_ref[...].astype(F32)) * bs + _sigmoid(ga_ref[...].astype(F32)) * ba).astype(o_ref.dtype)


def _merge(att, y, gates, w_ssm, w_att, bm):
    m = y.shape[0]
    bn = 512
    nj = D_MODEL // bn
    return pl.pallas_call(
        _merge_kernel,
        grid=(m // bm, nj),
        in_specs=[pl.BlockSpec((bm, ATT_OUT_WIDTH), lambda i, j: (i, 0)),
                  pl.BlockSpec((bm, D_INNER), lambda i, j: (i, 0)),
                  pl.BlockSpec((bm, bn), lambda i, j: (i, j)),
                  pl.BlockSpec((bm, bn), lambda i, j: (i, j + nj)),
                  pl.BlockSpec((D_INNER, bn), lambda i, j: (0, j)),
                  pl.BlockSpec((ATT_OUT_WIDTH, bn), lambda i, j: (0, j))],
        out_specs=pl.BlockSpec((bm, bn), lambda i, j: (i, j)),
        out_shape=jax.ShapeDtypeStruct((m, D_MODEL), BF16),
        compiler_params=_cparams("parallel", "arbitrary"),
        name="merge",
    )(att, y, gates, gates, w_ssm, w_att)


def _outproj_kernel(mg_ref, x_ref, gt_ref, gpost_ref, w_ref, o_ref):
    out = _dot(mg_ref[...], w_ref[...])
    o_ref[...] = x_ref[...] + gt_ref[...] * _rms(out, gpost_ref[...])


def _outproj(merged, x, gate, g_post, w_out, bm, rows_per_batch):
    m, d = x.shape
    return pl.pallas_call(
        _outproj_kernel,
        grid=(m // bm,),
        in_specs=[pl.BlockSpec((bm, d), lambda i: (i, 0)),
                  pl.BlockSpec((bm, d), lambda i: (i, 0)),
                  _mod_spec(gate, bm, rows_per_batch), _vec_spec(d),
                  pl.BlockSpec((d, d), lambda i: (0, 0))],
        out_specs=pl.BlockSpec((bm, d), lambda i: (i, 0)),
        out_shape=jax.ShapeDtypeStruct((m, d), F32),
        compiler_params=_cparams("parallel"),
        name="outproj",
    )(merged, x, gate, g_post.reshape(1, d), w_out)


def _layer(x3, mod, conv_prev, ssm_prev, kv_bufs, p, bm, kv_shift=None):
    b, t, d = x3.shape
    m = b * t
    prompt = kv_bufs is None
    x = x3.reshape(m, d)
    if prompt:
        mods = [mod[:, k].reshape(b, 1, d) for k in range(3 * N_SUB)]
    else:
        mods = [jnp.repeat(mod[:, k], t, axis=0) for k in range(3 * N_SUB)]

    x = _ffn(x, mods[0], mods[1], mods[2], p['g_pre_ffn1'], p['g_post_ffn1'], p['w_gu_ffn1'],
             p['w_down_ffn1'], bm, t)

    h = _prenorm(x, mods[3], mods[4], p['g_pre_mix'], bm, t)
    bmm = min(m, MM_BM)
    z = _mm(h, p['w_z'], BF16, bmm, MM_BN, "inproj_z")
    xbc = _mm(h, p['w_xbc'], BF16, bmm, MM_BN, "inproj_xbc")
    dtr = _mm(h, p['w_dt'], F32, bmm, LANES, "inproj_dt")
    qkv = _mm(h, p['w_qkv'], F32, bmm, ATT_QKV_WIDTH, "inproj_qkv")
    gates = _mm(h, p['w_gates'], BF16, bmm, MM_BN, "inproj_gates")

    xbc3 = xbc.reshape(b, t, CONV_DIM)
    if prompt:
        conv_prev = jnp.zeros((b, CONV_W - 1, CONV_DIM), F32)
        conv_new = xbc3[:, t - (CONV_W - 1):].astype(F32)
    else:
        conv_new = jnp.concatenate([conv_prev, xbc3.astype(F32)], axis=1)[:, -(CONV_W - 1):]
    h0 = None if prompt else ssm_prev.reshape(b, SSM_HEADS * SSM_HEAD_DIM, D_STATE)
    y, ssm_new = _ssd(xbc3, z.reshape(b, t, D_INNER), dtr.reshape(b, t, LANES), conv_prev, h0,
                      p['conv_w'], p['conv_b'], p['dt_bias'], p['a_log'], p['d_skip'], p['g_ssm_norm'])
    ssm_new = ssm_new.reshape(b, SSM_HEADS, SSM_HEAD_DIM, D_STATE)

    qkv3 = qkv.reshape(b, t, 3 * ATT_QKV_WIDTH)
    kv_new = []
    for gi, (window, dil) in enumerate(DIL_GROUPS):
        c0 = _qkv_unit(1, gi) * ATT_OUT_WIDTH
        rows = slice(t - min(window, t), t) if prompt else slice(0, t)
        if prompt:
            kv_new.append(_kv_rows(qkv3, gi, min(window, t)))
        else:
            lb = kv_bufs[gi].shape[1]
            kv_new.append((kv_bufs[gi].reshape(b, lb * 2 * ATT_HPG, ATT_HEAD_DIM),
                           qkv3[:, rows, c0:c0 + 2 * ATT_OUT_WIDTH].reshape(b, t * 2 * ATT_HPG, ATT_HEAD_DIM)))
    if prompt:
        att = _attn_prompt(qkv3)
    else:
        att = _attn_sample(qkv3, kv_bufs)

    merged = _merge(att.reshape(m, ATT_OUT_WIDTH), y.reshape(m, D_INNER), gates, p['w_ssm_proj'],
                    p['w_att_proj'], bmm)
    x = _outproj(merged, x, mods[5], p['g_post_mix'], p['w_out'], bm, t)

    x = _ffn(x, mods[6], mods[7], mods[8], p['g_pre_ffn2'], p['g_post_ffn2'], p['w_gu_ffn2'],
             p['w_down_ffn2'], bm, t, kv_shift)
    shifted = None
    if kv_shift is not None:
        x, shifted = x
    return x.reshape(b, t, d), conv_new, ssm_new, kv_new, shifted


def _cast_cols_kernel(wt_ref, o_ref):
    o_ref[...] = wt_ref[...].T.astype(o_ref.dtype)


def _cast_cols(wt, start, ncols, bn, out_order=None):
    k = wt.shape[1]
    assert ncols % bn == 0 and start % SUBLANES == 0

    def out_block(j):
        if out_order is None:
            return (0, j)
        dst = out_order[0]
        for src in range(1, len(out_order)):
            dst = jnp.where(j == src, out_order[src], dst)
        return (0, dst)

    return pl.pallas_call(
        _cast_cols_kernel,
        grid=(ncols // bn,),
        in_specs=[pl.BlockSpec((pl.Element(bn), pl.Element(k)), lambda j: (pl.multiple_of(start + j * bn, SUBLANES), 0))],
        out_specs=pl.BlockSpec((k, bn), out_block),
        out_shape=jax.ShapeDtypeStruct((k, ncols), BF16),
        compiler_params=_cparams("parallel"),
        name="cast_cols",
    )(wt)


def _prep_weights(w_gu_ffn1, w_down_ffn1, w_in, w_ssm_proj, w_att_proj, w_out, w_gu_ffn2, w_down_ffn2):
    o = 0
    segs = {}
    qkv_order = [_qkv_unit(part, gi) for part in range(3) for gi in range(N_DIL)]
    w_in_t = w_in.T
    for name, size, bn, order in (('z', D_INNER, 512, None), ('xbc', CONV_DIM, 512, None),
                                  ('dt', SSM_HEADS, LANES, None),
                                  ('qkv', 3 * ATT_QKV_WIDTH, ATT_OUT_WIDTH, qkv_order),
                                  ('gates', 2 * D_MODEL, 512, None)):
        segs[name] = _cast_cols(w_in_t, o, max(size, LANES), bn, order)
        o += size
    return {
        'w_gu_ffn1': w_gu_ffn1.astype(BF16), 'w_down_ffn1': w_down_ffn1.astype(BF16),
        'w_gu_ffn2': w_gu_ffn2.astype(BF16), 'w_down_ffn2': w_down_ffn2.astype(BF16),
        'w_z': segs['z'], 'w_xbc': segs['xbc'], 'w_dt': segs['dt'], 'w_qkv': segs['qkv'],
        'w_gates': segs['gates'],
        'w_ssm_proj': w_ssm_proj.astype(BF16), 'w_att_proj': w_att_proj.astype(BF16),
        'w_out': w_out.astype(BF16),
    }


def kernel(x_prompt, x_sample, c_prompt, c_sample, state_ssm, state_conv, cache_kv_w128, cache_kv_w512, cache_kv_w2048, w_ada, b_ada, g_pre_ffn1, g_post_ffn1, w_gu_ffn1, w_down_ffn1, g_pre_mix, g_post_mix, w_in, conv_w, conv_b, dt_bias, a_log, d_skip, g_ssm_norm, w_ssm_proj, w_att_proj, w_out, g_pre_ffn2, g_post_ffn2, w_gu_ffn2, w_down_ffn2):
    depth = w_ada.shape[0]
    kv_in = (cache_kv_w128, cache_kv_w512, cache_kv_w2048)
    bp = x_prompt.shape[0]
    yp, ys = x_prompt, x_sample
    outs = [[] for _ in range(10)]
    for l in range(depth):
        p = _prep_weights(w_gu_ffn1[l], w_down_ffn1[l], w_in[l], w_ssm_proj[l], w_att_proj[l], w_out[l],
                          w_gu_ffn2[l], w_down_ffn2[l])
        p.update({'g_pre_ffn1': g_pre_ffn1[l], 'g_post_ffn1': g_post_ffn1[l], 'g_pre_mix': g_pre_mix[l],
                  'g_post_mix': g_post_mix[l], 'g_pre_ffn2': g_pre_ffn2[l], 'g_post_ffn2': g_post_ffn2[l],
                  'conv_w': conv_w[l], 'conv_b': conv_b[l], 'dt_bias': dt_bias[l], 'a_log': a_log[l],
                  'd_skip': d_skip[l], 'g_ssm_norm': g_ssm_norm[l]})
        c_all = jnp.concatenate([c_prompt, c_sample], axis=0)
        mod = _ada(c_all, w_ada[l], b_ada[l]).reshape(c_all.shape[0], 3 * N_SUB, D_MODEL)
        ys, cs, ss, pending, _ = _layer(ys, mod[bp:], state_conv[l], state_ssm[l], tuple(kv[l] for kv in kv_in),
                                        p, ys.shape[0] * ys.shape[1])
        olds, news = [o for o, _ in pending], [n for _, n in pending]
        yp, cp, sp, kp, shifted = _layer(yp, mod[:bp], None, None, None, p, 512, (olds, news))
        ksm = [s.reshape(kv[l].shape) for s, kv in zip(shifted, kv_in)]
        for lst, val in zip(outs, (sp, ss, cp, cs, kp[0], ksm[0], kp[1], ksm[1], kp[2], ksm[2])):
            lst.append(val)
    return (yp, ys) + tuple(jnp.stack(o) for o in outs)
```

```python
import functools
import math

import numpy as np
import jax
import jax.numpy as jnp
from jax import lax
from jax.experimental import pallas as pl
from jax.experimental.pallas import tpu as pltpu

D_MODEL = 2048
SSM_HEAD_DIM = 64
D_INNER = 2 * D_MODEL
SSM_HEADS = D_INNER // SSM_HEAD_DIM
SSM_GROUPS = 8
SSM_HPG = SSM_HEADS // SSM_GROUPS
D_STATE = 128
CONV_W = 4
CONV_DIM = D_INNER + 2 * SSM_GROUPS * D_STATE
ATT_HEAD_DIM = 128
DIL_GROUPS = ((128, 1), (512, 4), (2048, 16))
N_DIL = len(DIL_GROUPS)
ATT_HPG = 4
ATT_HEADS = N_DIL * ATT_HPG
ATT_QKV_WIDTH = ATT_HEADS * ATT_HEAD_DIM
ATT_OUT_WIDTH = ATT_HPG * ATT_HEAD_DIM
ATT_BLOCK = 128
ALIBI_MAX_BIAS = 8.0
D_FF = ((8 * D_MODEL // 3 + 255) // 256) * 256
N_SUB = 3
EPS = 1e-6
NEG_INF = -1e30

LANES = 128
SUBLANES = 8
VMEM_LIMIT = 56 * 1024 * 1024
SSD_Q = 128
MM_BM = 1024
MM_BN = 1024

F32 = jnp.float32
BF16 = jnp.bfloat16


def _cparams(*sem):
    return pltpu.CompilerParams(dimension_semantics=sem, vmem_limit_bytes=VMEM_LIMIT)


def _qkv_unit(part, gi):
    return gi if part == 0 else N_DIL + 2 * gi + (part - 1)


def _alibi_slopes():
    i = np.arange(1, ATT_HEADS + 1, dtype=np.float32)
    return np.exp2(-ALIBI_MAX_BIAS * i / ATT_HEADS).astype(np.float32).reshape(N_DIL, ATT_HPG)


def _silu(x):
    return x * (1.0 / (1.0 + jnp.exp(-x)))


def _sigmoid(x):
    return 1.0 / (1.0 + jnp.exp(-x))


def _rms(x, g):
    return x * lax.rsqrt(jnp.mean(x * x, axis=-1, keepdims=True) + EPS) * g


def _split2(x):
    hi = x.astype(BF16)
    lo = (x - hi.astype(F32)).astype(BF16)
    return hi, lo


def _split3(x):
    hi = x.astype(BF16)
    r = x - hi.astype(F32)
    mid = r.astype(BF16)
    lo = (r - mid.astype(F32)).astype(BF16)
    return hi, mid, lo


def _dot(a, b):
    return jnp.dot(a, b, preferred_element_type=F32)


def _dot_nt(a, b):
    return lax.dot_general(a, b, (((1,), (1,)), ((), ())), preferred_element_type=F32)


def _dot_tn(a, b):
    return lax.dot_general(a, b, (((0,), (0,)), ((), ())), preferred_element_type=F32)


def _mod_spec(mod, bm, rows_per_batch):
    if mod.ndim == 3:
        bpb = rows_per_batch // bm
        return pl.BlockSpec((None, 1, D_MODEL), lambda i, *_: (i // bpb, 0, 0))
    return pl.BlockSpec((bm, D_MODEL), lambda i, *_: (i, 0))


def _vec_spec(n):
    return pl.BlockSpec((1, n), lambda *_: (0, 0))


def _ada_kernel(c_ref, w_ref, b_ref, o_ref):
    c = _silu(c_ref[...]).astype(BF16)
    o_ref[...] = _dot(c, w_ref[...].astype(BF16)) + b_ref[...]


def _ada(c, w, b):
    m, k = c.shape
    n = w.shape[1]
    bn = 1024
    return pl.pallas_call(
        _ada_kernel,
        grid=(n // bn,),
        in_specs=[pl.BlockSpec((m, k), lambda j: (0, 0)),
                  pl.BlockSpec((k, bn), lambda j: (0, j)),
                  pl.BlockSpec((1, bn), lambda j: (0, j))],
        out_specs=pl.BlockSpec((m, bn), lambda j: (0, j)),
        out_shape=jax.ShapeDtypeStruct((m, n), F32),
        compiler_params=_cparams("parallel"),
        name="ada",
    )(c, w, b.reshape(1, n))


KV_SHIFT_PARTS = 4
KV_SHIFT_EVERY = 2


def _kv_shift_step(n, old_refs, new_refs, out_refs, bufs, in_sem, out_sem, new_sem):
    nb = old_refs[0].shape[0]
    n_chunks = nb * KV_SHIFT_PARTS
    even = n % KV_SHIFT_EVERY == 0
    k = n // KV_SHIFT_EVERY
    n_bufs = len(old_refs)

    def chunk(c, kk):
        rows_new = new_refs[c].shape[1]
        rows = (old_refs[c].shape[1] - rows_new) // KV_SHIFT_PARTS
        bi = kk // KV_SHIFT_PARTS
        r0 = pl.multiple_of((kk % KV_SHIFT_PARTS) * rows, SUBLANES)
        slot = kk % 2
        load = pltpu.make_async_copy(old_refs[c].at[bi, pl.ds(r0 + rows_new, rows), :], bufs[c].at[slot],
                                     in_sem.at[slot, c])
        store = pltpu.make_async_copy(bufs[c].at[slot], out_refs[c].at[bi, pl.ds(r0, rows), :],
                                      out_sem.at[slot, c])
        return load, store

    def tail(c):
        rows_new = new_refs[c].shape[1]
        keep = old_refs[c].shape[1] - rows_new
        return pltpu.make_async_copy(new_refs[c], out_refs[c].at[:, pl.ds(keep, rows_new), :], new_sem.at[c])

    @pl.when(n == 0)
    def _():
        for c in range(n_bufs):
            tail(c).start()

    @pl.when(even & (k >= 1) & (k <= n_chunks))
    def _():
        for c in range(n_bufs):
            chunk(c, k - 1)[1].wait()

    @pl.when(even & (k < n_chunks))
    def _():
        for c in range(n_bufs):
            chunk(c, k)[0].start()

    @pl.when(jnp.logical_not(even) & (k < n_chunks))
    def _():
        for c in range(n_bufs):
            load, store = chunk(c, k)
            load.wait()
            store.start()

    @pl.when(n == KV_SHIFT_EVERY * n_chunks)
    def _():
        for c in range(n_bufs):
            tail(c).wait()


def _ffn_kernel(*refs, n_side):
    x_ref, sh_ref, sc_ref, gt_ref, gpre_ref, gpost_ref, wg_ref, wu_ref, wd_ref = refs[:9]
    side_in = refs[9:9 + 2 * n_side]
    o_ref = refs[9 + 2 * n_side]
    side_out = refs[10 + 2 * n_side:10 + 3 * n_side]
    h_scr, acc_scr = refs[10 + 3 * n_side:12 + 3 * n_side]
    j = pl.program_id(1)
    if n_side:
        rest = refs[12 + 3 * n_side:]
        bufs, (in_sem, out_sem, new_sem) = rest[:n_side], rest[n_side:]
        _kv_shift_step(pl.program_id(0) * pl.num_programs(1) + j, side_in[:n_side], side_in[n_side:], side_out,
                       bufs, in_sem, out_sem, new_sem)

    @pl.when(j == 0)
    def _():
        x = x_ref[...]
        h = _rms(x, gpre_ref[...]) * (1.0 + sc_ref[...]) + sh_ref[...]
        h_scr[...] = h.astype(BF16)
        acc_scr[...] = jnp.zeros_like(acc_scr)

    h = h_scr[...]
    g = _dot(h, wg_ref[...])
    u = _dot(h, wu_ref[...])
    a = (_silu(g) * u).astype(BF16)
    acc_scr[...] += _dot(a, wd_ref[...])

    @pl.when(j == pl.num_programs(1) - 1)
    def _():
        o_ref[...] = x_ref[...] + 0.5 * gt_ref[...] * _rms(acc_scr[...], gpost_ref[...])


def _ffn(x, shift, scale, gate, g_pre, g_post, w_gu, w_down, bm, rows_per_batch, kv_shift=None):
    m, d = x.shape
    ff = w_down.shape[0]
    bf = 512
    nj = ff // bf
    ms = lambda a: _mod_spec(a, bm, rows_per_batch)
    olds, news = kv_shift if kv_shift is not None else ([], [])
    n_side = len(olds)
    if n_side:
        assert (m // bm) * nj > KV_SHIFT_EVERY * olds[0].shape[0] * KV_SHIFT_PARTS
        for o, nw in zip(olds, news):
            assert (o.shape[1] - nw.shape[1]) % (KV_SHIFT_PARTS * SUBLANES) == 0 and nw.shape[1] % SUBLANES == 0
    any_spec = pl.BlockSpec(memory_space=pl.ANY)
    new_specs = [pl.BlockSpec(nw.shape, lambda i, j: (0, 0, 0)) for nw in news]
    side_scratch = []
    if n_side:
        assert KV_SHIFT_EVERY == 2
        side_scratch = [pltpu.VMEM((2, (o.shape[1] - nw.shape[1]) // KV_SHIFT_PARTS, o.shape[2]), o.dtype)
                        for o, nw in zip(olds, news)]
        side_scratch += [pltpu.SemaphoreType.DMA((2, n_side)), pltpu.SemaphoreType.DMA((2, n_side)),
                         pltpu.SemaphoreType.DMA((n_side,))]
    out = pl.pallas_call(
        functools.partial(_ffn_kernel, n_side=n_side),
        grid=(m // bm, nj),
        in_specs=[pl.BlockSpec((bm, d), lambda i, j: (i, 0)),
                  ms(shift), ms(scale), ms(gate), _vec_spec(d), _vec_spec(d),
                  pl.BlockSpec((d, bf), lambda i, j: (0, j)),
                  pl.BlockSpec((d, bf), lambda i, j: (0, j + nj)),
                  pl.BlockSpec((bf, d), lambda i, j: (j, 0))] + [any_spec] * n_side + new_specs,
        out_specs=[pl.BlockSpec((bm, d), lambda i, j: (i, 0))] + [any_spec] * n_side,
        out_shape=[jax.ShapeDtypeStruct((m, d), F32)] + [jax.ShapeDtypeStruct(o.shape, o.dtype) for o in olds],
        scratch_shapes=[pltpu.VMEM((bm, d), BF16), pltpu.VMEM((bm, d), F32)] + side_scratch,
        compiler_params=_cparams("arbitrary" if n_side else "parallel", "arbitrary"),
        name="ffn",
    )(x, shift, scale, gate, g_pre.reshape(1, d), g_post.reshape(1, d), w_gu, w_gu, w_down, *olds, *news)
    return (out[0], out[1:]) if n_side else out[0]


def _prenorm_kernel(x_ref, sh_ref, sc_ref, g_ref, o_ref):
    o_ref[...] = (_rms(x_ref[...], g_ref[...]) * (1.0 + sc_ref[...]) + sh_ref[...]).astype(o_ref.dtype)


def _prenorm(x, shift, scale, g, bm, rows_per_batch):
    m, d = x.shape
    ms = lambda a: _mod_spec(a, bm, rows_per_batch)
    return pl.pallas_call(
        _prenorm_kernel,
        grid=(m // bm,),
        in_specs=[pl.BlockSpec((bm, d), lambda i: (i, 0)), ms(shift), ms(scale), _vec_spec(d)],
        out_specs=pl.BlockSpec((bm, d), lambda i: (i, 0)),
        out_shape=jax.ShapeDtypeStruct((m, d), BF16),
        compiler_params=_cparams("parallel"),
        name="prenorm",
    )(x, shift, scale, g.reshape(1, d))


def _mm_kernel(x_ref, w_ref, o_ref):
    o_ref[...] = _dot(x_ref[...], w_ref[...]).astype(o_ref.dtype)


def _mm(x, w, out_dtype, bm, bn, name):
    m, k = x.shape
    n = w.shape[1]
    return pl.pallas_call(
        _mm_kernel,
        grid=(m // bm, n // bn),
        in_specs=[pl.BlockSpec((bm, k), lambda i, j: (i, 0)),
                  pl.BlockSpec((k, bn), lambda i, j: (0, j))],
        out_specs=pl.BlockSpec((bm, bn), lambda i, j: (i, j)),
        out_shape=jax.ShapeDtypeStruct((m, n), out_dtype),
        compiler_params=_cparams("parallel", "arbitrary"),
        name=name,
    )(x, w)


def _ssd_kernel(xbc_ref, xbcn_ref, z_ref, dtr_ref, prev_ref, h0_ref, cw_ref, cb_ref, dtb_ref, alog_ref,
                dskip_ref, gn_ref, e_ref, et_ref, y_ref, hout_ref,
                xp_scr, act_scr, y_scr, h_scr, dt_scr, *, q, nc, t_valid, has_init):
    c = pl.program_id(1)
    w8 = SUBLANES
    qo = xbc_ref.shape[0]
    slab = 512

    def rows8(ref, start):
        if ref.dtype == F32:
            return ref[start:start + w8, :]
        base = start // (2 * w8) * (2 * w8)
        return ref[base:base + 2 * w8, :].astype(F32)[start - base:start - base + w8]

    def conv_chunk(src_ref, act):
        if qo < q:
            assert qo == w8
            act[...] = jnp.zeros((q, CONV_DIM), F32)
        else:
            ri = lax.broadcasted_iota(jnp.int32, (q, q), 0)
            ci = lax.broadcasted_iota(jnp.int32, (q, q), 1)
            shift = jnp.concatenate([jnp.where(ri - ci == CONV_W - 1 - i, 1.0, 0.0) for i in range(CONV_W - 1)],
                                    axis=0).astype(BF16)
            for s in range(CONV_DIM // slab):
                cs = slice(s * slab, (s + 1) * slab)
                xb = src_ref[:, cs]
                sh = _dot(shift, xb)
                acc = cb_ref[:, cs] + cw_ref[CONV_W - 1:CONV_W, cs] * xb.astype(F32)
                for i in range(CONV_W - 1):
                    acc = acc + cw_ref[i:i + 1, cs] * sh[i * q:(i + 1) * q]
                act[:, cs] = _silu(acc)
        xp_scr[w8:2 * w8, :] = rows8(src_ref, 0)
        for s in range(CONV_DIM // slab):
            cs = slice(s * slab, (s + 1) * slab)
            acc = cb_ref[:, cs]
            for i in range(CONV_W):
                off = w8 - (CONV_W - 1) + i
                acc = acc + cw_ref[i:i + 1, cs] * xp_scr[off:off + w8, cs]
            act[0:w8, cs] = _silu(acc)
        xp_scr[0:w8, :] = rows8(src_ref, qo - w8)

    @pl.when(c == 0)
    def _():
        xp_scr[0:w8, :] = jnp.zeros((w8, CONV_DIM), F32)
        xp_scr[w8 - (CONV_W - 1):w8, :] = prev_ref[...]
        if has_init:
            h_scr[...] = h0_ref[...]
        else:
            h_scr[...] = jnp.zeros_like(h_scr)
        conv_chunk(xbc_ref, act_scr.at[0])

    if nc > 1:
        slot = c % 2
        conv_chunk(xbcn_ref, act_scr.at[1 - slot])
        act_cur = act_scr.at[slot]
    else:
        act_cur = act_scr.at[0]

    if qo < q:
        dt_scr[...] = jnp.zeros_like(dt_scr)
        dt_scr[0:qo, :] = dtr_ref[...]
    else:
        dt_scr[...] = dtr_ref[...]

    x = dt_scr[:, 0:SSM_HEADS] + dtb_ref[...]
    dt = jnp.maximum(x, 0.0) + jnp.log1p(jnp.exp(-jnp.abs(x)))
    if t_valid < q:
        rows = lax.broadcasted_iota(jnp.int32, (q, SSM_HEADS), 0)
        dt = jnp.where(rows < t_valid, dt, 0.0)
    a = -jnp.exp(alog_ref[...])
    la = dt * a
    ri = lax.broadcasted_iota(jnp.int32, (q, q), 0)
    ci = lax.broadcasted_iota(jnp.int32, (q, q), 1)
    causal = ri >= ci
    ltri = jnp.where(causal, 1.0, 0.0).astype(BF16)
    utri = jnp.where(ri <= ci, 1.0, 0.0).astype(BF16)
    parts = _split3(la)
    cum = _dot(ltri, parts[0]) + _dot(ltri, parts[1]) + _dot(ltri, parts[2])
    cum_t = _dot_tn(parts[0], utri) + _dot_tn(parts[1], utri) + _dot_tn(parts[2], utri)
    cum_last = cum[q - 1:q, :]
    dend = jnp.exp(cum_last - cum)
    ecum = jnp.exp(cum)

    stack = jnp.concatenate([dt, dt * dend, ecum], axis=0)
    s_hi, s_lo = _split2(stack)
    cdl = jnp.broadcast_to(jnp.exp(cum_t[:, q - 1:q]), (SSM_HEADS, LANES))
    c_hi, c_lo = _split2(cdl)

    lane = lax.broadcasted_iota(jnp.int32, (q, LANES), 1)
    lo_half = lane < SSM_HEAD_DIM
    causal_o = causal[0:qo]
    cum_o = cum[0:qo]
    gw = SSM_HPG * SSM_HEAD_DIM
    for g in range(SSM_GROUPS):
        gs = slice(g * gw, (g + 1) * gw)
        e_g = e_ref[:, gs]
        exp3 = _dot(s_hi, e_g) + _dot(s_lo, e_g)
        xs_g = act_cur[:, gs]
        xdt = (xs_g * exp3[0:q]).astype(BF16)
        xdd = (xs_g * exp3[q:2 * q]).astype(BF16)
        ecum_e = exp3[2 * q:2 * q + qo]
        bg = act_cur[:, D_INNER + g * D_STATE:D_INNER + (g + 1) * D_STATE].astype(BF16)
        cg = act_cur[:, D_INNER + SSM_GROUPS * D_STATE + g * D_STATE:
                     D_INNER + SSM_GROUPS * D_STATE + (g + 1) * D_STATE].astype(BF16)
        cbm = _dot_nt(cg[0:qo], bg)
        hg = h_scr[gs, :]
        yoff = _dot_nt(cg[0:qo], hg.astype(BF16))
        for pr in range(SSM_HPG // 2):
            h0 = g * SSM_HPG + 2 * pr
            a0 = cbm * jnp.exp(jnp.where(causal_o, cum_o[:, h0:h0 + 1] - cum_t[h0:h0 + 1, :], NEG_INF))
            a1 = cbm * jnp.exp(jnp.where(causal_o, cum_o[:, h0 + 1:h0 + 2] - cum_t[h0 + 1:h0 + 2, :], NEG_INF))
            lhs = jnp.concatenate([a0, a1], axis=1).astype(BF16)
            ps = slice(pr * LANES, (pr + 1) * LANES)
            x2 = xdt[:, ps]
            zero = jnp.zeros_like(x2)
            rhs = jnp.concatenate([jnp.where(lo_half, x2, zero), jnp.where(lo_half, zero, x2)], axis=0)
            yd = _dot(lhs, rhs)
            y_scr[:, g * gw + pr * LANES:g * gw + (pr + 1) * LANES] = yd + yoff[:, ps] * ecum_e[:, ps]
        st = _dot_tn(xdd, bg)
        et_g = et_ref[gs, :]
        cdec = _dot(et_g, c_hi) + _dot(et_g, c_lo)
        h_scr[gs, :] = hg * cdec + st

    y = y_scr[...] + act_cur[0:qo, 0:D_INNER] * dskip_ref[...]
    y = _rms(y * _silu(z_ref[...].astype(F32)), gn_ref[...])
    y_ref[...] = y.astype(y_ref.dtype)

    @pl.when(c == pl.num_programs(1) - 1)
    def _():
        hout_ref[...] = h_scr[...]


def _ssd(xbc, z, dtr, conv_prev, h0, conv_w, conv_b, dt_bias, a_log, d_skip, g_norm):
    b, t, _ = xbc.shape
    q = SSD_Q
    t_valid = min(t, q)
    if t >= q:
        assert t % q == 0
        nc, bt = t // q, q
    else:
        bt = -(-t // SUBLANES) * SUBLANES
        padt = lambda v: jnp.pad(v.astype(F32), ((0, 0), (0, bt - t), (0, 0)))
        xbc, z, dtr = padt(xbc), padt(z), padt(dtr)
        nc = 1
    has_init = h0 is not None
    if not has_init:
        h0 = jnp.zeros((b, SUBLANES, D_STATE), F32)
    hp = SSM_HEADS * SSM_HEAD_DIM
    e = jnp.asarray(np.kron(np.eye(SSM_HEADS, dtype=np.float32), np.ones((1, SSM_HEAD_DIM), np.float32)), BF16)
    et = jnp.asarray(np.kron(np.eye(SSM_HEADS, dtype=np.float32), np.ones((SSM_HEAD_DIM, 1), np.float32)), BF16)
    dskip_e = jnp.repeat(d_skip, SSM_HEAD_DIM).reshape(1, D_INNER)
    kern = functools.partial(_ssd_kernel, q=q, nc=nc, t_valid=t_valid, has_init=has_init)
    h0_block = (None, hp, D_STATE) if has_init else (None, SUBLANES, D_STATE)
    y, h = pl.pallas_call(
        kern,
        grid=(b, nc),
        in_specs=[pl.BlockSpec((None, bt, CONV_DIM), lambda i, c: (i, c, 0)),
                  pl.BlockSpec((None, bt, CONV_DIM), lambda i, c: (i, jnp.minimum(c + 1, nc - 1), 0)),
                  pl.BlockSpec((None, bt, D_INNER), lambda i, c: (i, c, 0)),
                  pl.BlockSpec((None, bt, LANES), lambda i, c: (i, c, 0)),
                  pl.BlockSpec((None, CONV_W - 1, CONV_DIM), lambda i, c: (i, 0, 0)),
                  pl.BlockSpec(h0_block, lambda i, c: (i, 0, 0)),
                  pl.BlockSpec((CONV_W, CONV_DIM), lambda i, c: (0, 0)),
                  _vec_spec(CONV_DIM), _vec_spec(SSM_HEADS), _vec_spec(SSM_HEADS),
                  _vec_spec(D_INNER), _vec_spec(D_INNER),
                  pl.BlockSpec((SSM_HEADS, D_INNER), lambda i, c: (0, 0)),
                  pl.BlockSpec((D_INNER, SSM_HEADS), lambda i, c: (0, 0))],
        out_specs=[pl.BlockSpec((None, bt, D_INNER), lambda i, c: (i, c, 0)),
                   pl.BlockSpec((None, hp, D_STATE), lambda i, c: (i, 0, 0))],
        out_shape=[jax.ShapeDtypeStruct((b, bt * nc, D_INNER), BF16),
                   jax.ShapeDtypeStruct((b, hp, D_STATE), F32)],
        scratch_shapes=[pltpu.VMEM((2 * SUBLANES, CONV_DIM), F32),
                        pltpu.VMEM((min(nc, 2), q, CONV_DIM), F32),
                        pltpu.VMEM((bt, D_INNER), F32),
                        pltpu.VMEM((hp, D_STATE), F32),
                        pltpu.VMEM((q, LANES), F32)],
        compiler_params=_cparams("parallel", "arbitrary"),
        name="ssd",
    )(xbc, xbc, z, dtr, conv_prev, h0, conv_w, conv_b.reshape(1, CONV_DIM), dt_bias.reshape(1, SSM_HEADS),
      a_log.reshape(1, SSM_HEADS), dskip_e, g_norm.reshape(1, D_INNER), e, et)
    return y[:, :t], h


def _attn_prompt_kernel(*refs, slopes, seq):
    qkv_refs = refs[:3 * N_DIL]
    o_ref, og_scr, lg_scr = refs[3 * N_DIL:]
    blk = ATT_BLOCK
    h = pl.program_id(1)
    qi = lax.broadcasted_iota(jnp.int32, (blk, 2 * blk), 0)
    ki = lax.broadcasted_iota(jnp.int32, (blk, 2 * blk), 1)
    dist = qi + blk - ki
    scale = ATT_HEAD_DIM ** -0.5
    for gi, (window, dil) in enumerate(DIL_GROUPS):
        q_ref, k_ref, v_ref = qkv_refs[gi], qkv_refs[N_DIL + gi], qkv_refs[2 * N_DIL + gi]
        slope = slopes[gi][0]
        for hh in range(1, ATT_HPG):
            slope = jnp.where(h == hh, slopes[gi][hh], slope)
        valid2 = (dist >= 0) & (dist <= window // dil)
        bias_any = jnp.where(valid2, -slope * (dil * dist).astype(F32), NEG_INF)
        bias_first = jnp.where(ki >= blk, bias_any, NEG_INF)
        nb = seq // dil // blk

        def rows(start, size):
            return pl.ds(start, size) if dil == 1 else pl.ds(start, size, stride=dil)

        starts = [r + dil * n * blk for r in range(dil) for n in range(nb)]
        firsts = [n == 0 for r in range(dil) for n in range(nb)]

        def band(ref, q0, first):
            if first:
                cur = ref[rows(q0, blk), :].astype(BF16)
                return jnp.concatenate([cur, cur], axis=0)
            return ref[rows(q0 - dil * blk, 2 * blk), :].astype(BF16)

        qs = jnp.stack([q_ref[rows(q0, blk), :].astype(BF16) for q0 in starts])
        ks = jnp.stack([band(k_ref, q0, f) for q0, f in zip(starts, firsts)])
        vs = jnp.stack([band(v_ref, q0, f) for q0, f in zip(starts, firsts)])
        bias = jnp.stack([bias_first if f else bias_any for f in firsts])
        s = jnp.einsum('bqd,bkd->bqk', qs, ks, preferred_element_type=F32) * scale + bias
        m = jnp.max(s, axis=-1, keepdims=True)
        p = jnp.exp(s - m)
        l = jnp.sum(p, axis=-1, keepdims=True)
        o = jnp.einsum('bqk,bkd->bqd', p.astype(BF16), vs, preferred_element_type=F32) / l
        lse = m + jnp.log(l)
        for idx, q0 in enumerate(starts):
            og_scr[gi, rows(q0, blk), :] = o[idx]
            lg_scr[gi, rows(q0, blk), :] = jnp.broadcast_to(lse[idx], (blk, ATT_HEAD_DIM))

    l0, l1, l2 = lg_scr[0], lg_scr[1], lg_scr[2]
    mm = jnp.maximum(jnp.maximum(l0, l1), l2)
    w0, w1, w2 = jnp.exp(l0 - mm), jnp.exp(l1 - mm), jnp.exp(l2 - mm)
    att = (w0 * og_scr[0] + w1 * og_scr[1] + w2 * og_scr[2]) / (w0 + w1 + w2)
    o_ref[...] = att.astype(o_ref.dtype)


def _attn_prompt(qkv):
    b, s, w = qkv.shape
    for window, dil in DIL_GROUPS:
        assert s % (dil * ATT_BLOCK) == 0
    hd = ATT_HEAD_DIM
    specs = [pl.BlockSpec((None, s, hd), functools.partial(lambda i, h, c: (i, 0, c + h), c=_qkv_unit(part, gi) * ATT_HPG))
             for part in range(3) for gi in range(N_DIL)]
    kern = functools.partial(_attn_prompt_kernel, seq=s,
                             slopes=[[float(v) for v in row] for row in _alibi_slopes()])
    return pl.pallas_call(
        kern,
        grid=(b, ATT_HPG),
        in_specs=specs,
        out_specs=pl.BlockSpec((None, s, hd), lambda i, h: (i, 0, h)),
        out_shape=jax.ShapeDtypeStruct((b, s, ATT_OUT_WIDTH), BF16),
        scratch_shapes=[pltpu.VMEM((N_DIL, s, hd), F32), pltpu.VMEM((N_DIL, s, hd), F32)],
        compiler_params=_cparams("parallel", "arbitrary"),
        name="attn_prompt",
    )(*([qkv] * (3 * N_DIL)))


def _attn_sample_kernel(qkv_ref, c0_ref, c1_ref, c2_ref, o_ref, *, t, slopes):
    scale = ATT_HEAD_DIM ** -0.5
    caches = (c0_ref, c1_ref, c2_ref)
    nk = ATT_BLOCK
    step = lax.broadcasted_iota(jnp.int32, (nk, ATT_HPG, 1), 0)
    hid = lax.broadcasted_iota(jnp.int32, (ATT_HPG, 1), 0)
    for tq in range(t):
        outs, lses = [], []
        for gi, (window, dil) in enumerate(DIL_GROUPS):
            kmax = window // dil
            slope = jnp.full((ATT_HPG, 1), slopes[gi][0], F32)
            for hh in range(1, ATT_HPG):
                slope = jnp.where(hid == hh, slopes[gi][hh], slope)
            q = qkv_ref[tq, _qkv_unit(0, gi)]
            cref = caches[gi]
            if dil == 1:
                kc, vc = cref[:, 0], cref[:, 1]
                dist = kmax + tq - step
                new = [(j, tq - j) for j in range(tq + 1)]
            else:
                kc, vc = cref[:, tq, 0], cref[:, tq, 1]
                dist = kmax - step
                new = [(tq, 0)]
            s = jnp.sum(kc * q[None], axis=-1, keepdims=True) * scale - slope[None] * (dil * dist).astype(F32)
            s = jnp.where(dist <= kmax, s, NEG_INF)
            s_new = [jnp.sum(qkv_ref[j, _qkv_unit(1, gi)] * q, axis=-1, keepdims=True) * scale - slope * float(dil * d)
                     for j, d in new]
            m = jnp.max(s, axis=0)
            for sn in s_new:
                m = jnp.maximum(m, sn)
            p = jnp.exp(s - m[None])
            l = jnp.sum(p, axis=0)
            o = jnp.sum(p * vc, axis=0)
            for (j, _), sn in zip(new, s_new):
                pn = jnp.exp(sn - m)
                l = l + pn
                o = o + pn * qkv_ref[j, _qkv_unit(2, gi)]
            outs.append(o / l)
            lses.append(m + jnp.log(l))
        mm = jnp.maximum(jnp.maximum(lses[0], lses[1]), lses[2])
        ws = [jnp.exp(x - mm) for x in lses]
        o_ref[tq] = (ws[0] * outs[0] + ws[1] * outs[1] + ws[2] * outs[2]) / (ws[0] + ws[1] + ws[2])


def _attn_sample(qkv, caches):
    b, t, w = qkv.shape
    qkv5 = qkv.reshape(b, t, 3 * N_DIL, ATT_HPG, ATT_HEAD_DIM)
    specs = [pl.BlockSpec((None, t, 3 * N_DIL, ATT_HPG, ATT_HEAD_DIM), lambda i: (i, 0, 0, 0, 0))]
    views = []
    for (window, dil), cache in zip(DIL_GROUPS, caches):
        lb = cache.shape[1]
        assert lb == window and lb // dil == ATT_BLOCK and (dil == 1 or dil >= t)
        if dil == 1:
            views.append(cache)
            specs.append(pl.BlockSpec((None, lb, 2, ATT_HPG, ATT_HEAD_DIM), lambda i: (i, 0, 0, 0, 0)))
        else:
            views.append(cache.reshape(b, lb // dil, dil, 2, ATT_HPG, ATT_HEAD_DIM))
            specs.append(pl.BlockSpec((None, lb // dil, t, 2, ATT_HPG, ATT_HEAD_DIM), lambda i: (i, 0, 0, 0, 0, 0)))
    kern = functools.partial(_attn_sample_kernel, t=t,
                             slopes=[[float(v) for v in row] for row in _alibi_slopes()])
    att = pl.pallas_call(
        kern,
        grid=(b,),
        in_specs=specs,
        out_specs=pl.BlockSpec((None, t, ATT_HPG, ATT_HEAD_DIM), lambda i: (i, 0, 0, 0)),
        out_shape=jax.ShapeDtypeStruct((b, t, ATT_HPG, ATT_HEAD_DIM), F32),
        compiler_params=_cparams("parallel"),
        name="attn_sample",
    )(qkv5, *views)
    return att.reshape(b, t, ATT_OUT_WIDTH)


def _kv_rows_kernel(k_ref, v_ref, o_ref):
    br = k_ref.shape[0]
    per_row = 2 * ATT_HPG
    for part, ref in enumerate((k_ref, v_ref)):
        for h in range(ATT_HPG):
            o_ref[pl.ds(part * ATT_HPG + h, br, stride=per_row), :] = ref[:, h * ATT_HEAD_DIM:(h + 1) * ATT_HEAD_DIM]


def _kv_rows(qkv, gi, w):
    b, s, _ = qkv.shape
    br = min(w, 512)
    assert w % br == 0 and s % br == 0
    r0 = (s - w) // br
    per_row = 2 * ATT_HPG
    ku, vu = _qkv_unit(1, gi), _qkv_unit(2, gi)
    out = pl.pallas_call(
        _kv_rows_kernel,
        grid=(b, w // br),
        in_specs=[pl.BlockSpec((None, br, ATT_OUT_WIDTH), lambda i, r: (i, r0 + r, ku)),
                  pl.BlockSpec((None, br, ATT_OUT_WIDTH), lambda i, r: (i, r0 + r, vu))],
        out_specs=pl.BlockSpec((None, br * per_row, ATT_HEAD_DIM), lambda i, r: (i, r, 0)),
        out_shape=jax.ShapeDtypeStruct((b, w * per_row, ATT_HEAD_DIM), F32),
        compiler_params=_cparams("parallel", "parallel"),
        name="kv_rows",
    )(qkv, qkv)
    return out.reshape(b, w, 2, ATT_HPG, ATT_HEAD_DIM)


def _merge_kernel(att_ref, y_ref, gs_ref, ga_ref, ws_ref, wa_ref, o_ref):
    bs = _dot(y_ref[...], ws_ref[...])
    ba = _dot(att_ref[...].astype(BF16), wa_ref[...])
    o_ref[...] = (_sigmoid(gs_ref[...].astype(F32)) * bs + _sigmoid(ga_ref[...].astype(F32)) * ba).astype(o_ref.dtype)


def _merge(att, y, gates, w_ssm, w_att, bm):
    m = y.shape[0]
    bn = 512
    nj = D_MODEL // bn
    return pl.pallas_call(
        _merge_kernel,
        grid=(m // bm, nj),
        in_specs=[pl.BlockSpec((bm, ATT_OUT_WIDTH), lambda i, j: (i, 0)),
                  pl.BlockSpec((bm, D_INNER), lambda i, j: (i, 0)),
                  pl.BlockSpec((bm, bn), lambda i, j: (i, j)),
                  pl.BlockSpec((bm, bn), lambda i, j: (i, j + nj)),
                  pl.BlockSpec((D_INNER, bn), lambda i, j: (0, j)),
                  pl.BlockSpec((ATT_OUT_WIDTH, bn), lambda i, j: (0, j))],
        out_specs=pl.BlockSpec((bm, bn), lambda i, j: (i, j)),
        out_shape=jax.ShapeDtypeStruct((m, D_MODEL), BF16),
        compiler_params=_cparams("parallel", "arbitrary"),
        name="merge",
    )(att, y, gates, gates, w_ssm, w_att)


def _outproj_kernel(mg_ref, x_ref, gt_ref, gpost_ref, w_ref, o_ref):
    out = _dot(mg_ref[...], w_ref[...])
    o_ref[...] = x_ref[...] + gt_ref[...] * _rms(out, gpost_ref[...])


def _outproj(merged, x, gate, g_post, w_out, bm, rows_per_batch):
    m, d = x.shape
    return pl.pallas_call(
        _outproj_kernel,
        grid=(m // bm,),
        in_specs=[pl.BlockSpec((bm, d), lambda i: (i, 0)),
                  pl.BlockSpec((bm, d), lambda i: (i, 0)),
                  _mod_spec(gate, bm, rows_per_batch), _vec_spec(d),
                  pl.BlockSpec((d, d), lambda i: (0, 0))],
        out_specs=pl.BlockSpec((bm, d), lambda i: (i, 0)),
        out_shape=jax.ShapeDtypeStruct((m, d), F32),
        compiler_params=_cparams("parallel"),
        name="outproj",
    )(merged, x, gate, g_post.reshape(1, d), w_out)


def _layer(x3, mod, conv_prev, ssm_prev, kv_bufs, p, bm, kv_shift=None):
    b, t, d = x3.shape
    m = b * t
    prompt = kv_bufs is None
    x = x3.reshape(m, d)
    if prompt:
        mods = [mod[:, k].reshape(b, 1, d) for k in range(3 * N_SUB)]
    else:
        mods = [jnp.repeat(mod[:, k], t, axis=0) for k in range(3 * N_SUB)]

    x = _ffn(x, mods[0], mods[1], mods[2], p['g_pre_ffn1'], p['g_post_ffn1'], p['w_gu_ffn1'],
             p['w_down_ffn1'], bm, t)

    h = _prenorm(x, mods[3], mods[4], p['g_pre_mix'], bm, t)
    bmm = min(m, MM_BM)
    z = _mm(h, p['w_z'], BF16, bmm, MM_BN, "inproj_z")
    xbc = _mm(h, p['w_xbc'], BF16, bmm, MM_BN, "inproj_xbc")
    dtr = _mm(h, p['w_dt'], F32, bmm, LANES, "inproj_dt")
    qkv = _mm(h, p['w_qkv'], F32, bmm, ATT_QKV_WIDTH, "inproj_qkv")
    gates = _mm(h, p['w_gates'], BF16, bmm, MM_BN, "inproj_gates")

    xbc3 = xbc.reshape(b, t, CONV_DIM)
    if prompt:
        conv_prev = jnp.zeros((b, CONV_W - 1, CONV_DIM), F32)
        conv_new = xbc3[:, t - (CONV_W - 1):].astype(F32)
    else:
        conv_new = jnp.concatenate([conv_prev, xbc3.astype(F32)], axis=1)[:, -(CONV_W - 1):]
    h0 = None if prompt else ssm_prev.reshape(b, SSM_HEADS * SSM_HEAD_DIM, D_STATE)
    y, ssm_new = _ssd(xbc3, z.reshape(b, t, D_INNER), dtr.reshape(b, t, LANES), conv_prev, h0,
                      p['conv_w'], p['conv_b'], p['dt_bias'], p['a_log'], p['d_skip'], p['g_ssm_norm'])
    ssm_new = ssm_new.reshape(b, SSM_HEADS, SSM_HEAD_DIM, D_STATE)

    qkv3 = qkv.reshape(b, t, 3 * ATT_QKV_WIDTH)
    kv_new = []
    for gi, (window, dil) in enumerate(DIL_GROUPS):
        c0 = _qkv_unit(1, gi) * ATT_OUT_WIDTH
        rows = slice(t - min(window, t), t) if prompt else slice(0, t)
        if prompt:
            kv_new.append(_kv_rows(qkv3, gi, min(window, t)))
        else:
            lb = kv_bufs[gi].shape[1]
            kv_new.append((kv_bufs[gi].reshape(b, lb * 2 * ATT_HPG, ATT_HEAD_DIM),
                           qkv3[:, rows, c0:c0 + 2 * ATT_OUT_WIDTH].reshape(b, t * 2 * ATT_HPG, ATT_HEAD_DIM)))
    if prompt:
        att = _attn_prompt(qkv3)
    else:
        att = _attn_sample(qkv3, kv_bufs)

    merged = _merge(att.reshape(m, ATT_OUT_WIDTH), y.reshape(m, D_INNER), gates, p['w_ssm_proj'],
                    p['w_att_proj'], bmm)
    x = _outproj(merged, x, mods[5], p['g_post_mix'], p['w_out'], bm, t)

    x = _ffn(x, mods[6], mods[7], mods[8], p['g_pre_ffn2'], p['g_post_ffn2'], p['w_gu_ffn2'],
             p['w_down_ffn2'], bm, t, kv_shift)
    shifted = None
    if kv_shift is not None:
        x, shifted = x
    return x.reshape(b, t, d), conv_new, ssm_new, kv_new, shifted


def _cast_cols_kernel(wt_ref, o_ref):
    o_ref[...] = wt_ref[...].T.astype(o_ref.dtype)


def _cast_cols(wt, start, ncols, bn, out_order=None):
    k = wt.shape[1]
    assert ncols % bn == 0 and start % SUBLANES == 0

    def out_block(j):
        if out_order is None:
            return (0, j)
        dst = out_order[0]
        for src in range(1, len(out_order)):
            dst = jnp.where(j == src, out_order[src], dst)
        return (0, dst)

    return pl.pallas_call(
        _cast_cols_kernel,
        grid=(ncols // bn,),
        in_specs=[pl.BlockSpec((pl.Element(bn), pl.Element(k)), lambda j: (pl.multiple_of(start + j * bn, SUBLANES), 0))],
        out_specs=pl.BlockSpec((k, bn), out_block),
        out_shape=jax.ShapeDtypeStruct((k, ncols), BF16),
        compiler_params=_cparams("parallel"),
        name="cast_cols",
    )(wt)


def _prep_weights(w_gu_ffn1, w_down_ffn1, w_in, w_ssm_proj, w_att_proj, w_out, w_gu_ffn2, w_down_ffn2):
    o = 0
    segs = {}
    qkv_order = [_qkv_unit(part, gi) for part in range(3) for gi in range(N_DIL)]
    w_in_t = w_in.T
    for name, size, bn, order in (('z', D_INNER, 512, None), ('xbc', CONV_DIM, 512, None),
                                  ('dt', SSM_HEADS, LANES, None),
                                  ('qkv', 3 * ATT_QKV_WIDTH, ATT_OUT_WIDTH, qkv_order),
                                  ('gates', 2 * D_MODEL, 512, None)):
        segs[name] = _cast_cols(w_in_t, o, max(size, LANES), bn, order)
        o += size
    return {
        'w_gu_ffn1': w_gu_ffn1.astype(BF16), 'w_down_ffn1': w_down_ffn1.astype(BF16),
        'w_gu_ffn2': w_gu_ffn2.astype(BF16), 'w_down_ffn2': w_down_ffn2.astype(BF16),
        'w_z': segs['z'], 'w_xbc': segs['xbc'], 'w_dt': segs['dt'], 'w_qkv': segs['qkv'],
        'w_gates': segs['gates'],
        'w_ssm_proj': w_ssm_proj.astype(BF16), 'w_att_proj': w_att_proj.astype(BF16),
        'w_out': w_out.astype(BF16),
    }


def kernel(x_prompt, x_sample, c_prompt, c_sample, state_ssm, state_conv, cache_kv_w128, cache_kv_w512, cache_kv_w2048, w_ada, b_ada, g_pre_ffn1, g_post_ffn1, w_gu_ffn1, w_down_ffn1, g_pre_mix, g_post_mix, w_in, conv_w, conv_b, dt_bias, a_log, d_skip, g_ssm_norm, w_ssm_proj, w_att_proj, w_out, g_pre_ffn2, g_post_ffn2, w_gu_ffn2, w_down_ffn2):
    depth = w_ada.shape[0]
    kv_in = (cache_kv_w128, cache_kv_w512, cache_kv_w2048)
    bp = x_prompt.shape[0]
    yp, ys = x_prompt, x_sample
    outs = [[] for _ in range(10)]
    for l in range(depth):
        p = _prep_weights(w_gu_ffn1[l], w_down_ffn1[l], w_in[l], w_ssm_proj[l], w_att_proj[l], w_out[l],
                          w_gu_ffn2[l], w_down_ffn2[l])
        p.update({'g_pre_ffn1': g_pre_ffn1[l], 'g_post_ffn1': g_post_ffn1[l], 'g_pre_mix': g_pre_mix[l],
                  'g_post_mix': g_post_mix[l], 'g_pre_ffn2': g_pre_ffn2[l], 'g_post_ffn2': g_post_ffn2[l],
                  'conv_w': conv_w[l], 'conv_b': conv_b[l], 'dt_bias': dt_bias[l], 'a_log': a_log[l],
                  'd_skip': d_skip[l], 'g_ssm_norm': g_ssm_norm[l]})
        c_all = jnp.concatenate([c_prompt, c_sample], axis=0)
        mod = _ada(c_all, w_ada[l], b_ada[l]).reshape(c_all.shape[0], 3 * N_SUB, D_MODEL)
        ys, cs, ss, pending, _ = _layer(ys, mod[bp:], state_conv[l], state_ssm[l], tuple(kv[l] for kv in kv_in),
                                        p, ys.shape[0] * ys.shape[1])
        olds, news = [o for o, _ in pending], [n for _, n in pending]
        yp, cp, sp, kp, shifted = _layer(yp, mod[:bp], None, None, None, p, 512, (olds, news))
        ksm = [s.reshape(kv[l].shape) for s, kv in zip(shifted, kv_in)]
        for lst, val in zip(outs, (sp, ss, cp, cs, kp[0], ksm[0], kp[1], ksm[1], kp[2], ksm[2])):
            lst.append(val)
    return (yp, ys) + tuple(jnp.stack(o) for o in outs)
```
